```python
import jax, jax.numpy as jnp
from jax import lax
import numpy as np

D_MODEL = 1024
BATCH = 8
SEQ = 4096
DEPTH = 2

N_MEM = 256
GLA_HEADS = 6
GLA_DK = 64
GLA_DV = 128
GLA_K = GLA_HEADS * GLA_DK
GLA_V = GLA_HEADS * GLA_DV
GLA_GATE_RANK = 16
GLA_TAU = 16.0
GLA_CHUNK = 64
SB_HEADS = 12
SB_DIM = 64
SB_W = SB_HEADS * SB_DIM
SB_BLOCK = 128
MEM_HEADS = 4
MEM_DIM = 64
MEM_W = MEM_HEADS * MEM_DIM
PEER_HEADS = 8
PEER_KEYS = 128
PEER_EXPERTS = PEER_KEYS * PEER_KEYS
PEER_TOPK = 16
PEER_QHALF = 128
PEER_TOKEN_BLOCK = 128
N_A_LAYERS = DEPTH // 2
N_B_LAYERS = DEPTH - N_A_LAYERS
DEEPNORM_ALPHA = (2.0 * DEPTH) ** 0.25
DEEPNORM_BETA = (8.0 * DEPTH) ** -0.25
EPS = 1e-5
A_IN_SPLITS = [GLA_K, 2 * GLA_K, 2 * GLA_K + GLA_V, 2 * GLA_K + 2 * GLA_V, 2 * GLA_K + 2 * GLA_V + GLA_GATE_RANK]
A_IN_W = 2 * GLA_K + 2 * GLA_V + GLA_GATE_RANK + MEM_W
B_IN_W = SB_W + MEM_W

kernel_name = "yoco_gla_stickbreaking_peer_trunk"


def layer_norm(x, g, b):
    xf = x.astype(jnp.float32)
    mu = jnp.mean(xf, axis=-1, keepdims=True)
    var = jnp.mean(jnp.square(xf - mu), axis=-1, keepdims=True)
    return ((xf - mu) * lax.rsqrt(var + EPS) * g + b).astype(x.dtype)


def to_heads(t, n_heads, d):
    B, T, _ = t.shape
    return t.reshape(B, T, n_heads, d).transpose(0, 2, 1, 3)


def from_heads(t):
    B, H, T, d = t.shape
    return t.transpose(0, 2, 1, 3).reshape(B, T, H * d)


def memory_attention(qm, mem, w_mem_kv):
    B, T, _ = qm.shape
    kv = mem @ w_mem_kv
    km, vm = jnp.split(kv, [MEM_W], axis=-1)
    q = qm.reshape(B, T, MEM_HEADS, MEM_DIM)
    k = km.reshape(B, -1, MEM_HEADS, MEM_DIM)
    v = vm.reshape(B, -1, MEM_HEADS, MEM_DIM)
    s = jnp.einsum('bthd,bmhd->bhtm', q, k).astype(jnp.float32) * (MEM_DIM ** -0.5)
    p = jax.nn.softmax(s, axis=-1).astype(v.dtype)
    return jnp.einsum('bhtm,bmhd->bthd', p, v).reshape(B, T, MEM_W)


def gla_chunked(q, k, v, log_g):
    B, H, T, dk = q.shape
    dv = v.shape[-1]
    n = T // GLA_CHUNK
    c = lambda t: t.astype(jnp.float32).reshape(B, H, n, GLA_CHUNK, t.shape[-1])
    q, k, v, log_g = c(q), c(k), c(v), c(log_g)
    b = jnp.cumsum(log_g, axis=-2)
    b_last = b[..., -1, :]
    q_dec = q * jnp.exp(b)
    k_inv = k * jnp.exp(-b)
    causal = jnp.tril(jnp.ones((GLA_CHUNK, GLA_CHUNK), dtype=bool))
    attn = jnp.where(causal, jnp.einsum('bhncd,bhnsd->bhncs', q_dec, k_inv), 0.0)
    o_intra = jnp.einsum('bhncs,bhnsv->bhncv', attn, v)
    k_to_end = k * jnp.exp(b_last[..., None, :] - b)
    du = jnp.einsum('bhncd,bhncv->bhndv', k_to_end, v)

    def step(S, inp):
        decay, inc = inp
        return decay[..., None] * S + inc, S

    S0 = jnp.zeros((B, H, dk, dv), jnp.float32)
    _, S_start = lax.scan(step, S0, (jnp.moveaxis(jnp.exp(b_last), 2, 0), jnp.moveaxis(du, 2, 0)))
    S_start = jnp.moveaxis(S_start, 0, 2)
    o_inter = jnp.einsum('bhncd,bhndv->bhncv', q_dec, S_start)
    return (o_intra + o_inter).reshape(B, H, T, dv)


def gla_layer_mixer(x, mem, w_in, w_gate2, b_gate, head_g, w_mem_kv, w_out):
    proj = x @ w_in
    q, k, v, r, glr, qm = jnp.split(proj, A_IN_SPLITS, axis=-1)
    log_g = jax.nn.log_sigmoid((glr @ w_gate2 + b_gate).astype(jnp.float32)) / GLA_TAU
    o = gla_chunked(to_heads(q, GLA_HEADS, GLA_DK) * (GLA_DK ** -0.5),
                    to_heads(k, GLA_HEADS, GLA_DK),
                    to_heads(v, GLA_HEADS, GLA_DV),
                    to_heads(log_g, GLA_HEADS, GLA_DK))
    o = o * lax.rsqrt(jnp.mean(jnp.square(o), axis=-1, keepdims=True) + EPS) * head_g[None, :, None, :]
    o = from_heads(o).astype(x.dtype) * jax.nn.silu(r)
    m = memory_attention(qm, mem, w_mem_kv)
    return jnp.concatenate([o, m], axis=-1) @ w_out


def stick_breaking(q, k, v):
    B, H, T, d = q.shape
    scale = d ** -0.5
    outs = []
    for i in range(T // SB_BLOCK):
        t0, t1 = i * SB_BLOCK, (i + 1) * SB_BLOCK
        z = jnp.einsum('bhtd,bhsd->bhts', q[:, :, t0:t1], k[:, :, :t1]).astype(jnp.float32) * scale
        t_pos = t0 + jnp.arange(SB_BLOCK)[:, None]
        s_pos = jnp.arange(t1)[None, :]
        mask = s_pos < t_pos
        log_not = jnp.where(mask, jax.nn.log_sigmoid(-z), 0.0)
        log_A = jax.nn.log_sigmoid(z) + lax.cumsum(log_not, axis=3, reverse=True) - log_not
        A = jnp.where(mask, jnp.exp(log_A), 0.0).astype(v.dtype)
        outs.append(jnp.einsum('bhts,bhsd->bhtd', A, v[:, :, :t1]))
    return jnp.concatenate(outs, axis=2)


def sb_layer_mixer(x, mem, k_sb, v_sb, w_in, w_mem_kv, w_out):
    q, qm = jnp.split(x @ w_in, [SB_W], axis=-1)
    o = from_heads(stick_breaking(to_heads(q, SB_HEADS, SB_DIM), k_sb, v_sb))
    m = memory_attention(qm, mem, w_mem_kv)
    return jnp.concatenate([o, m], axis=-1) @ w_out


def peer(x, w_q, subkeys, u_tab, v_tab):
    B, T, D = x.shape
    N = B * T
    xf = x.reshape(N, D)
    q = (xf @ w_q).reshape(N, PEER_HEADS, 2, PEER_QHALF)
    s = jnp.einsum('nhpd,hpkd->nhpk', q, subkeys).astype(jnp.float32)
    top_s, top_i = lax.top_k(s, PEER_TOPK)
    cand_s = (top_s[:, :, 0, :, None] + top_s[:, :, 1, None, :]).reshape(N, PEER_HEADS, PEER_TOPK * PEER_TOPK)
    cand_i = (top_i[:, :, 0, :, None] * PEER_KEYS + top_i[:, :, 1, None, :]).reshape(N, PEER_HEADS, PEER_TOPK * PEER_TOPK)
    best_s, best_pos = lax.top_k(cand_s, PEER_TOPK)
    idx = jnp.take_along_axis(cand_i, best_pos, axis=-1)
    g = jax.nn.softmax(best_s, axis=-1).astype(x.dtype)
    nb = N // PEER_TOKEN_BLOCK
    xb = xf.reshape(nb, PEER_TOKEN_BLOCK, D)
    ib = idx.reshape(nb, PEER_TOKEN_BLOCK, PEER_HEADS, PEER_TOPK)
    gb = g.reshape(nb, PEER_TOKEN_BLOCK, PEER_HEADS, PEER_TOPK)

    def apply_experts(args):
        xt, it, gt = args
        act = jax.nn.gelu(jnp.einsum('nd,nhkd->nhk', xt, u_tab[it]), approximate=False)
        return jnp.einsum('nhk,nhkd->nd', gt * act, v_tab[it])

    return lax.map(apply_experts, (xb, ib, gb)).reshape(B, T, D)


def setup_inputs(seed: int = 0) -> dict:
    key = jax.random.key(seed)
    ks = jax.random.split(key, 20)
    nrm = lambda k, shape, scale: jax.random.normal(k, shape, jnp.float32) * scale
    ds = D_MODEL ** -0.5
    beta = DEEPNORM_BETA
    a_col_scale = jnp.concatenate([
        jnp.ones((2 * GLA_K,), jnp.float32),
        jnp.full((GLA_V,), beta, jnp.float32),
        jnp.ones((GLA_V + GLA_GATE_RANK + MEM_W,), jnp.float32)])
    mem_col_scale = jnp.concatenate([jnp.ones((MEM_W,), jnp.float32), jnp.full((MEM_W,), beta, jnp.float32)])
    sb_col_scale = jnp.concatenate([jnp.ones((SB_W,), jnp.float32), jnp.full((SB_W,), beta, jnp.float32)])
    return {
        "x": nrm(ks[0], (BATCH, SEQ, D_MODEL), 1.0),
        "mem": nrm(ks[1], (BATCH, N_MEM, D_MODEL), 1.0),
        "a_w_in": nrm(ks[2], (N_A_LAYERS, D_MODEL, A_IN_W), ds) * a_col_scale,
        "a_w_gate2": nrm(ks[3], (N_A_LAYERS, GLA_GATE_RANK, GLA_K), GLA_GATE_RANK ** -0.5),
        "a_b_gate": nrm(ks[4], (N_A_LAYERS, GLA_K), 0.1),
        "a_head_g": 1.0 + nrm(ks[5], (N_A_LAYERS, GLA_HEADS, GLA_DV), 0.05),
        "a_w_mem_kv": nrm(ks[6], (N_A_LAYERS, D_MODEL, 2 * MEM_W), ds) * mem_col_scale,
        "a_w_out": nrm(ks[7], (N_A_LAYERS, GLA_V + MEM_W, D_MODEL), (GLA_V + MEM_W) ** -0.5 * beta),
        "b_w_in": nrm(ks[8], (N_B_LAYERS, D_MODEL, B_IN_W), ds),
        "b_w_mem_kv": nrm(ks[9], (N_B_LAYERS, D_MODEL, 2 * MEM_W), ds) * mem_col_scale,
        "b_w_out": nrm(ks[10], (N_B_LAYERS, SB_W + MEM_W, D_MODEL), (SB_W + MEM_W) ** -0.5 * beta),
        "sb_w_kv": nrm(ks[11], (D_MODEL, 2 * SB_W), ds) * sb_col_scale,
        "peer_w_q": nrm(ks[12], (DEPTH, D_MODEL, PEER_HEADS * 2 * PEER_QHALF), ds),
        "peer_subkeys": nrm(ks[13], (DEPTH, PEER_HEADS, 2, PEER_KEYS, PEER_QHALF), PEER_QHALF ** -0.5),
        "peer_u": nrm(ks[14], (DEPTH, PEER_EXPERTS, D_MODEL), ds),
        "peer_v": nrm(ks[15], (DEPTH, PEER_EXPERTS, D_MODEL), beta * PEER_HEADS ** -0.5),
        "ln_g": 1.0 + nrm(ks[16], (DEPTH, 2, D_MODEL), 0.05),
        "ln_b": nrm(ks[17], (DEPTH, 2, D_MODEL), 0.02),
    }


def reference(x, mem, a_w_in, a_w_gate2, a_b_gate, a_head_g, a_w_mem_kv, a_w_out,
              b_w_in, b_w_mem_kv, b_w_out, sb_w_kv, peer_w_q, peer_subkeys, peer_u, peer_v,
              ln_g, ln_b):
    h = x
    k_sb = None
    v_sb = None
    for layer in range(DEPTH):
        if layer < N_A_LAYERS:
            i = layer
            mix = gla_layer_mixer(h, mem, a_w_in[i], a_w_gate2[i], a_b_gate[i], a_head_g[i],
                                  a_w_mem_kv[i], a_w_out[i])
        else:
            if layer == N_A_LAYERS:
                ksb, vsb = jnp.split(h @ sb_w_kv, [SB_W], axis=-1)
                k_sb = to_heads(ksb, SB_HEADS, SB_DIM)
                v_sb = to_heads(vsb, SB_HEADS, SB_DIM)
            j = layer - N_A_LAYERS
            mix = sb_layer_mixer(h, mem, k_sb, v_sb, b_w_in[j], b_w_mem_kv[j], b_w_out[j])
        h = layer_norm(DEEPNORM_ALPHA * h + mix, ln_g[layer, 0], ln_b[layer, 0])
        ffn = peer(h, peer_w_q[layer], peer_subkeys[layer], peer_u[layer], peer_v[layer])
        h = layer_norm(DEEPNORM_ALPHA * h + ffn, ln_g[layer, 1], ln_b[layer, 1])
    return h
```

```python
import functools

import jax
import jax.numpy as jnp
from jax import lax
from jax.experimental import pallas as pl
from jax.experimental.pallas import tpu as pltpu

F32 = jnp.float32
BF16 = jnp.bfloat16
I32 = jnp.int32

LANES = 128
SUBLANES = 8
VMEM_LIMIT_DEFAULT = 48 * 1024 * 1024
VMEM_LIMIT_TABLE = 56 * 1024 * 1024

D_MODEL = 1024
N_MEM = 256
GLA_HEADS = 6
GLA_DK = 64
GLA_DV = 128
GLA_K = GLA_HEADS * GLA_DK
GLA_V = GLA_HEADS * GLA_DV
GLA_GATE_RANK = 16
GLA_TAU = 16.0
GLA_CHUNK = 64
SB_HEADS = 12
SB_DIM = 64
SB_W = SB_HEADS * SB_DIM
SB_BLOCK = 128
MEM_HEADS = 4
MEM_DIM = 64
MEM_W = MEM_HEADS * MEM_DIM
PEER_HEADS = 8
PEER_KEYS = 128
PEER_EXPERTS = PEER_KEYS * PEER_KEYS
PEER_TOPK = 16
PEER_QHALF = 128
PEER_PICKS = PEER_HEADS * PEER_TOPK
DEPTH = 2
DEEPNORM_ALPHA = (2.0 * DEPTH) ** 0.25
EPS = 1e-5

ROW_CHUNKS = D_MODEL // LANES
ROW_WORDS = ROW_CHUNKS // 2

NT_DIMS = (((1,), (1,)), ((), ()))
TN_DIMS = (((0,), (0,)), ((), ()))


def _params(semantics, vmem=VMEM_LIMIT_DEFAULT):
    return pltpu.CompilerParams(dimension_semantics=semantics, vmem_limit_bytes=vmem)


def _log_sigmoid(z):
    return jnp.minimum(z, 0.0) - jnp.log1p(jnp.exp(-jnp.abs(z)))


def _dot(a, b):
    return jnp.dot(a, b, preferred_element_type=F32)


def _split_bf16(x, parts):
    out = []
    r = x
    for _ in range(parts):
        p = r.astype(BF16)
        out.append(p)
        r = r - p.astype(F32)
    return out


def _layer_norm(y, g, b):
    mu = jnp.mean(y, axis=-1, keepdims=True)
    yc = y - mu
    var = jnp.mean(yc * yc, axis=-1, keepdims=True)
    return yc * lax.rsqrt(var + EPS) * g + b


def _matmul_kernel(x_ref, w_ref, o_ref):
    o_ref[...] = _dot(x_ref[...].astype(BF16), w_ref[...])


def _matmul(x, w, tm=512, tn=256):
    m, k = x.shape
    n = w.shape[1]
    tm = min(tm, m)
    tn = min(tn, n)
    return pl.pallas_call(
        _matmul_kernel,
        grid=(m // tm, n // tn),
        in_specs=[pl.BlockSpec((tm, k), lambda i, j: (i, 0)),
                  pl.BlockSpec((k, tn), lambda i, j: (0, j))],
        out_specs=pl.BlockSpec((tm, tn), lambda i, j: (i, j)),
        out_shape=jax.ShapeDtypeStruct((m, n), F32),
        compiler_params=_params(("parallel", "arbitrary")),
        name="matmul",
    )(x, w)


GLA_ROWS = 256


def _gla_kernel(q_ref, k_ref, v_ref, r_ref, glr_ref, wg_ref, bg_ref, hg_ref, o_ref, st_ref):
    @pl.when(pl.program_id(2) == 0)
    def _():
        st_ref[...] = jnp.zeros_like(st_ref)

    c_rows = GLA_CHUNK
    lane = lax.broadcasted_iota(I32, (c_rows, LANES), 1)
    head_masks = (lane < GLA_DK, lane >= GLA_DK)
    row = lax.broadcasted_iota(I32, (c_rows, c_rows), 0)
    col = lax.broadcasted_iota(I32, (c_rows, c_rows), 1)
    causal = row >= col
    tril = causal.astype(BF16)
    wg = wg_ref[...]
    bg = bg_ref[...]
    hg = hg_ref[...]
    for c in range(GLA_ROWS // c_rows):
        rows = slice(c * c_rows, (c + 1) * c_rows)
        z = _dot(glr_ref[rows, :].astype(BF16), wg) + bg
        lg = _log_sigmoid(z) * (1.0 / GLA_TAU)
        bc = sum(_dot(tril, part) for part in _split_bf16(lg, 3))
        bl = bc[c_rows - 1:c_rows, :]
        q = q_ref[rows, :] * (GLA_DK ** -0.5)
        k = k_ref[rows, :]
        qd = q * jnp.exp(bc)
        ki = (k * jnp.exp(-bc)).astype(BF16)
        ke = k * jnp.exp(bl - bc)
        dec = jnp.exp(bl)
        for a in range(2):
            cols = slice(a * GLA_DV, (a + 1) * GLA_DV)
            qa = jnp.where(head_masks[a], qd, 0.0).astype(BF16)
            kea = jnp.where(head_masks[a], ke, 0.0).astype(BF16)
            attn = lax.dot_general(qa, ki, NT_DIMS, preferred_element_type=F32)
            attn = jnp.where(causal, attn, 0.0).astype(BF16)
            va = v_ref[rows, cols].astype(BF16)
            st = st_ref[a]
            o = _dot(attn, va) + lax.dot_general(qa, st.astype(BF16), NT_DIMS,
                                                 preferred_element_type=F32)
            st_ref[a] = st * dec + lax.dot_general(va, kea, TN_DIMS, preferred_element_type=F32)
            ms = jnp.mean(o * o, axis=-1, keepdims=True)
            r = r_ref[rows, cols]
            silu = r / (1.0 + jnp.exp(-r))
            o_ref[rows, cols] = o * lax.rsqrt(ms + EPS) * hg[:, cols] * silu


def _gla(proj, wg, bg, hg, batch, seq):
    n = proj.shape[0]
    tr = min(GLA_ROWS, seq)
    nt = seq // tr
    rowmap = lambda b, j, t: b * nt + t
    pair_w = 2 * GLA_DV
    return pl.pallas_call(
        _gla_kernel,
        grid=(batch, GLA_HEADS // 2, nt),
        in_specs=[
            pl.BlockSpec((tr, LANES), lambda b, j, t: (rowmap(b, j, t), j)),
            pl.BlockSpec((tr, LANES), lambda b, j, t: (rowmap(b, j, t), GLA_K // LANES + j)),
            pl.BlockSpec((tr, pair_w), lambda b, j, t: (rowmap(b, j, t), 2 * GLA_K // pair_w + j)),
            pl.BlockSpec((tr, pair_w),
                         lambda b, j, t: (rowmap(b, j, t), (2 * GLA_K + GLA_V) // pair_w + j)),
            pl.BlockSpec((tr, pair_w),
                         lambda b, j, t: (rowmap(b, j, t), (2 * GLA_K + 2 * GLA_V + MEM_W) // pair_w)),
            pl.BlockSpec((pair_w, LANES), lambda b, j, t: (0, j)),
            pl.BlockSpec((1, LANES), lambda b, j, t: (0, j)),
            pl.BlockSpec((1, pair_w), lambda b, j, t: (0, j)),
        ],
        out_specs=pl.BlockSpec((tr, pair_w), lambda b, j, t: (rowmap(b, j, t), j)),
        out_shape=jax.ShapeDtypeStruct((n, GLA_V), F32),
        scratch_shapes=[pltpu.VMEM((2, GLA_DV, LANES), F32)],
        compiler_params=_params(("parallel", "parallel", "arbitrary")),
        name="gla",
    )(proj, proj, proj, proj, proj, wg, bg, hg)


MEM_ROWS = 512


def _mem_attn_kernel(q_ref, kv_ref, o_ref):
    tq = q_ref.shape[0]
    lane = lax.broadcasted_iota(I32, (tq, LANES), 1)
    lo_head = lane < MEM_DIM
    for j in range(MEM_W // LANES):
        cols = slice(j * LANES, (j + 1) * LANES)
        q2 = q_ref[:, cols]
        k2 = kv_ref[:, cols].astype(BF16)
        v2 = kv_ref[:, MEM_W + j * LANES:MEM_W + (j + 1) * LANES].astype(BF16)
        outs = []
        for a in range(2):
            qa = jnp.where(lo_head if a == 0 else ~lo_head, q2, 0.0).astype(BF16)
            s = lax.dot_general(qa, k2, NT_DIMS, preferred_element_type=F32) * (MEM_DIM ** -0.5)
            e = jnp.exp(s - jnp.max(s, axis=-1, keepdims=True))
            p = e / jnp.sum(e, axis=-1, keepdims=True)
            outs.append(_dot(p.astype(BF16), v2))
        o_ref[:, cols] = jnp.where(lo_head, outs[0], outs[1])


def _mem_attn(proj, q_col_block, kv, batch, seq):
    n = proj.shape[0]
    tq = min(MEM_ROWS, seq)
    nt = seq // tq
    return pl.pallas_call(
        _mem_attn_kernel,
        grid=(batch, nt),
        in_specs=[pl.BlockSpec((tq, MEM_W), lambda b, t: (b * nt + t, q_col_block)),
                  pl.BlockSpec((N_MEM, 2 * MEM_W), lambda b, t: (b, 0))],
        out_specs=pl.BlockSpec((tq, MEM_W), lambda b, t: (b * nt + t, 0)),
        out_shape=jax.ShapeDtypeStruct((n, MEM_W), F32),
        compiler_params=_params(("parallel", "parallel")),
        name="mem_attn",
    )(proj, kv)


def _sb_kernel(q_ref, k_ref, v_ref, o_ref):
    i = pl.program_id(2)
    blk = SB_BLOCK
    lane = lax.broadcasted_iota(I32, (blk, LANES), 1)
    row = lax.broadcasted_iota(I32, (blk, LANES), 0)
    lo_head = lane < SB_DIM
    strict = lane < row
    suffix_and_total = jnp.concatenate(
        [(row > lane).astype(BF16), jnp.ones((blk, LANES), BF16)], axis=1)
    q2 = q_ref[...]

    def block(j, qa, run, acc, diag):
        start = pl.multiple_of(j * blk, blk)
        kj = k_ref[pl.ds(start, blk), :].astype(BF16)
        vj = v_ref[pl.ds(start, blk), :].astype(BF16)
        z = lax.dot_general(qa, kj, NT_DIMS, preferred_element_type=F32) * (SB_DIM ** -0.5)
        ls = _log_sigmoid(z)
        ln = ls - z
        if diag:
            ln = jnp.where(strict, ln, 0.0)
        cs = sum(_dot(part, suffix_and_total) for part in _split_bf16(ln, 2))
        a_w = jnp.exp(ls + cs[:, :LANES] + run)
        if diag:
            a_w = jnp.where(strict, a_w, 0.0)
        return run + cs[:, LANES:], acc + _dot(a_w.astype(BF16), vj)

    outs = []
    for a in range(2):
        qa = jnp.where(lo_head if a == 0 else ~lo_head, q2, 0.0).astype(BF16)
        zeros = jnp.zeros((blk, LANES), F32)
        run, acc = block(i, qa, zeros, zeros, True)

        def body(t, carry, qa=qa):
            return block(i - 1 - t, qa, carry[0], carry[1], False)

        run, acc = lax.fori_loop(0, i, body, (run, acc))
        outs.append(acc)
    o_ref[...] = jnp.where(lo_head, outs[0], outs[1])


def _stick_breaking(qproj, kv, batch, seq):
    n = qproj.shape[0]
    nq = seq // SB_BLOCK
    pairs = SB_W // LANES
    return pl.pallas_call(
        _sb_kernel,
        grid=(batch, pairs, nq),
        in_specs=[pl.BlockSpec((SB_BLOCK, LANES), lambda b, j, i: (b * nq + i, j)),
                  pl.BlockSpec((seq, LANES), lambda b, j, i: (b, j)),
                  pl.BlockSpec((seq, LANES), lambda b, j, i: (b, pairs + j))],
        out_specs=pl.BlockSpec((SB_BLOCK, LANES), lambda b, j, i: (b * nq + i, j)),
        out_shape=jax.ShapeDtypeStruct((n, SB_W), F32),
        compiler_params=_params(("parallel", "parallel", "arbitrary")),
        name="stick_breaking",
    )(qproj, kv, kv)


LN_ROWS = 256


def _proj_ln_kernel(a1_ref, a2_ref, h_ref, w1_ref, w2_ref, g_ref, b_ref, o_ref):
    mix = _dot(a1_ref[...].astype(BF16), w1_ref[...]) + _dot(a2_ref[...].astype(BF16), w2_ref[...])
    o_ref[...] = _layer_norm(DEEPNORM_ALPHA * h_ref[...] + mix, g_ref[...], b_ref[...])


def _proj_ln(a1, a2, h, w1, w2, g, b):
    n = h.shape[0]
    tm = min(LN_ROWS, n)
    full = lambda arr: pl.BlockSpec(arr.shape, lambda i: (0, 0))
    rows = lambda arr: pl.BlockSpec((tm, arr.shape[1]), lambda i: (i, 0))
    return pl.pallas_call(
        _proj_ln_kernel,
        grid=(n // tm,),
        in_specs=[rows(a1), rows(a2), rows(h), full(w1), full(w2), full(g), full(b)],
        out_specs=rows(h),
        out_shape=jax.ShapeDtypeStruct(h.shape, F32),
        compiler_params=_params(("parallel",)),
        name="proj_ln",
    )(a1, a2, h, w1, w2, g, b)


def _res_ln_kernel(f_ref, h_ref, g_ref, b_ref, o_ref):
    o_ref[...] = _layer_norm(DEEPNORM_ALPHA * h_ref[...] + f_ref[...], g_ref[...], b_ref[...])


def _res_ln(f, h, g, b):
    n = h.shape[0]
    tm = min(512, n)
    full = lambda arr: pl.BlockSpec(arr.shape, lambda i: (0, 0))
    rows = lambda arr: pl.BlockSpec((tm, arr.shape[1]), lambda i: (i, 0))
    return pl.pallas_call(
        _res_ln_kernel,
        grid=(n // tm,),
        in_specs=[rows(f), rows(h), full(g), full(b)],
        out_specs=rows(h),
        out_shape=jax.ShapeDtypeStruct(h.shape, F32),
        compiler_params=_params(("parallel",)),
        name="res_ln",
    )(f, h, g, b)


TOPK_TOKENS = 128


def _top16_rows(vals, payload=None):
    k = vals.shape[0]
    iota = lax.broadcasted_iota(I32, vals.shape, 0).astype(F32)
    out_v, out_p = [], []
    for _ in range(PEER_TOPK):
        m = jnp.max(vals, axis=0, keepdims=True)
        am = jnp.min(jnp.where(vals == m, iota, float(k)), axis=0, keepdims=True)
        hit = iota == am
        out_v.append(m)
        if payload is None:
            out_p.append(am)
        else:
            out_p.append(jnp.max(jnp.where(hit, payload, -1.0), axis=0, keepdims=True))
        vals = jnp.where(hit, -jnp.inf, vals)
    return out_v, out_p


def _peer_topk_kernel(q_ref, sk_ref, idx_ref, g_ref):
    tops = []
    for p in range(2):
        qp = q_ref[:, p * PEER_QHALF:(p + 1) * PEER_QHALF].astype(BF16)
        s_t = lax.dot_general(sk_ref[0, p], qp, NT_DIMS, preferred_element_type=F32)
        tops.append(_top16_rows(s_t))
    (s0, i0), (s1, i1) = tops
    s1_all = jnp.concatenate(s1, axis=0)
    i1_all = jnp.concatenate(i1, axis=0)
    cand_s = jnp.concatenate([s0[i] + s1_all for i in range(PEER_TOPK)], axis=0)
    cand_i = jnp.concatenate([i0[i] * float(PEER_KEYS) + i1_all for i in range(PEER_TOPK)], axis=0)
    best_s, best_i = _top16_rows(cand_s, cand_i)
    e = [jnp.exp(s - best_s[0]) for s in best_s]
    denom = sum(e)
    g_ref[...] = jnp.concatenate(e, axis=0) / denom
    idx_ref[...] = jnp.concatenate(best_i, axis=0).astype(I32)


def _peer_topk(qp, subkeys):
    n = qp.shape[0]
    tt = min(TOPK_TOKENS, n)
    spec_out = pl.BlockSpec((PEER_TOPK, tt), lambda i, h: (h, i))
    return pl.pallas_call(
        _peer_topk_kernel,
        grid=(n // tt, PEER_HEADS),
        in_specs=[pl.BlockSpec((tt, 2 * PEER_QHALF), lambda i, h: (i, h)),
                  pl.BlockSpec((1, 2, PEER_KEYS, PEER_QHALF), lambda i, h: (h, 0, 0, 0))],
        out_specs=[spec_out, spec_out],
        out_shape=[jax.ShapeDtypeStruct((PEER_PICKS, n), I32),
                   jax.ShapeDtypeStruct((PEER_PICKS, n), F32)],
        compiler_params=_params(("parallel", "parallel")),
        name="peer_topk",
    )(qp, subkeys)


def _pack_kernel(x_ref, o_ref):
    o_ref[...] = pltpu.bitcast(x_ref[...].astype(BF16), I32)


def _pack_rows(x2d):
    r = x2d.shape[0]
    tr = min(4096, r)
    return pl.pallas_call(
        _pack_kernel,
        grid=(r // tr,),
        in_specs=[pl.BlockSpec((tr, LANES), lambda i: (i, 0))],
        out_specs=pl.BlockSpec((tr // 2, LANES), lambda i: (i, 0)),
        out_shape=jax.ShapeDtypeStruct((r // 2, LANES), I32),
        compiler_params=_params(("parallel",)),
        name="pack_bf16",
    )(x2d)


def _pack_table(tab):
    e = tab.shape[0]
    return _pack_rows(tab.reshape(e * ROW_CHUNKS, LANES)).reshape(e, ROW_WORDS, LANES)


PEER_TOKENS = 64
PAIRS = PEER_PICKS // 2


def _peer_u_kernel(idx_ref, x_ref, tab_ref, o_ref):
    ones = jnp.ones((LANES, LANES), BF16)
    lane = lax.broadcasted_iota(I32, (ROW_CHUNKS, LANES), 1)

    def token(n, carry):
        xw = pltpu.bitcast(x_ref[n], BF16)
        base = n * PEER_PICKS
        prods = []
        for k in range(PAIRS):
            e0 = idx_ref[0, 0, base + 2 * k]
            e1 = idx_ref[0, 0, base + 2 * k + 1]
            w = jnp.concatenate([tab_ref[e0], tab_ref[e1]], axis=0)
            prods.append(pltpu.bitcast(w, BF16) * xw)
        z = _dot(jnp.concatenate(prods, axis=0), ones)
        acc = jnp.zeros((ROW_CHUNKS, LANES), F32)
        for e in range(PEER_PICKS):
            acc = acc + jnp.where(lane == e, z[e * ROW_CHUNKS:(e + 1) * ROW_CHUNKS, :], 0.0)
        o_ref[n] = jnp.sum(acc, axis=0, keepdims=True)
        return carry

    lax.fori_loop(0, o_ref.shape[0], token, 0)


def _peer_v_kernel(idx_ref, w_ref, tab_ref, o_ref):
    def token(n, carry):
        base = n * PEER_PICKS
        n_acc = 4
        acc_lo = [jnp.zeros((ROW_WORDS, LANES), F32) for _ in range(n_acc)]
        acc_hi = [jnp.zeros((ROW_WORDS, LANES), F32) for _ in range(n_acc)]
        for e in range(PEER_PICKS):
            row = tab_ref[idx_ref[0, 0, base + e]]
            wt = w_ref[0, 0, base + e]
            lo = lax.bitcast_convert_type(row << 16, F32)
            hi = lax.bitcast_convert_type(row & jnp.int32(-65536), F32)
            acc_lo[e % n_acc] = acc_lo[e % n_acc] + wt * lo
            acc_hi[e % n_acc] = acc_hi[e % n_acc] + wt * hi
        o_ref[n, 0] = sum(acc_lo)
        o_ref[n, 1] = sum(acc_hi)
        return carry

    lax.fori_loop(0, o_ref.shape[0], token, 0)


def _smem_rows(arr, tb):
    n = arr.shape[0]
    return arr.reshape(n // tb, 1, tb * PEER_PICKS)


def _table_spec(tab):
    return pl.BlockSpec(tab.shape, lambda i: (0, 0, 0), pipeline_mode=pl.Buffered(1))


def _peer_u(idx, xw, tab):
    n = idx.shape[0]
    tb = min(PEER_TOKENS, n)
    smem = pl.BlockSpec((1, 1, tb * PEER_PICKS), lambda i: (i, 0, 0), memory_space=pltpu.SMEM)
    out = pl.pallas_call(
        _peer_u_kernel,
        grid=(n // tb,),
        in_specs=[smem, pl.BlockSpec((tb, SUBLANES, LANES), lambda i: (i, 0, 0)), _table_spec(tab)],
        out_specs=pl.BlockSpec((tb, 1, LANES), lambda i: (i, 0, 0)),
        out_shape=jax.ShapeDtypeStruct((n, 1, LANES), F32),
        compiler_params=_params(("arbitrary",), VMEM_LIMIT_TABLE),
        name="peer_u",
    )(_smem_rows(idx, tb), xw, tab)
    return out.reshape(n, LANES)


def _peer_v(idx, w, tab):
    n = idx.shape[0]
    tb = min(PEER_TOKENS, n)
    smem = pl.BlockSpec((1, 1, tb * PEER_PICKS), lambda i: (i, 0, 0), memory_space=pltpu.SMEM)
    out = pl.pallas_call(
        _peer_v_kernel,
        grid=(n // tb,),
        in_specs=[smem, smem, _table_spec(tab)],
        out_specs=pl.BlockSpec((tb, 2, ROW_WORDS, LANES), lambda i: (i, 0, 0, 0)),
        out_shape=jax.ShapeDtypeStruct((n, 2, ROW_WORDS, LANES), F32),
        compiler_params=_params(("arbitrary",), VMEM_LIMIT_TABLE),
        name="peer_v",
    )(_smem_rows(idx, tb), _smem_rows(w, tb), tab)
    return out.transpose(0, 2, 1, 3).reshape(n, D_MODEL)


def _gate_act_kernel(a_ref, g_ref, o_ref):
    a = a_ref[...]
    o_ref[...] = g_ref[...] * (0.5 * a * (1.0 + lax.erf(a * (2.0 ** -0.5))))


def _gate_act(act, g):
    n = act.shape[0]
    tm = min(2048, n)
    spec = pl.BlockSpec((tm, LANES), lambda i: (i, 0))
    return pl.pallas_call(
        _gate_act_kernel,
        grid=(n // tm,),
        in_specs=[spec, spec],
        out_specs=spec,
        out_shape=jax.ShapeDtypeStruct(act.shape, F32),
        compiler_params=_params(("parallel",)),
        name="peer_gate_act",
    )(act, g)


def _peer(h, w_q, subkeys, u_tab, v_tab):
    n = h.shape[0]
    qp = _matmul(h, w_q.astype(BF16))
    idx_t, g_t = _peer_topk(qp, subkeys.astype(BF16))
    idx = idx_t.T
    g = g_t.T
    hr = h.reshape(n, ROW_CHUNKS, LANES)
    xw = _pack_rows(jnp.concatenate([hr, hr], axis=1).reshape(n * 2 * ROW_CHUNKS, LANES))
    act = _peer_u(idx, xw.reshape(n, SUBLANES, LANES), _pack_table(u_tab))
    w = _gate_act(act, g)
    return _peer_v(idx, w, _pack_table(v_tab))


def kernel(x, mem, a_w_in, a_w_gate2, a_b_gate, a_head_g, a_w_mem_kv, a_w_out, b_w_in, b_w_mem_kv,
           b_w_out, sb_w_kv, peer_w_q, peer_subkeys, peer_u, peer_v, ln_g, ln_b):
    batch, seq, d = x.shape
    n = batch * seq
    h = x.reshape(n, d)
    memf = mem.reshape(batch * N_MEM, d)
    pad_w = 2 * GLA_DV
    s = [GLA_K, 2 * GLA_K, 2 * GLA_K + GLA_V, 2 * GLA_K + 2 * GLA_V,
         2 * GLA_K + 2 * GLA_V + GLA_GATE_RANK]

    w_in = a_w_in[0]
    w_in_r = jnp.concatenate(
        [w_in[:, :s[3]], w_in[:, s[4]:], w_in[:, s[3]:s[4]],
         jnp.zeros((d, pad_w - GLA_GATE_RANK), F32)], axis=1).astype(BF16)
    proj = _matmul(h, w_in_r)
    wg = jnp.concatenate([a_w_gate2[0], jnp.zeros((pad_w - GLA_GATE_RANK, GLA_K), F32)],
                         axis=0).astype(BF16)
    o = _gla(proj, wg, a_b_gate[0].reshape(1, GLA_K), a_head_g[0].reshape(1, GLA_V), batch, seq)
    kv_mem = _matmul(memf, a_w_mem_kv[0].astype(BF16))
    m = _mem_attn(proj, (2 * GLA_K + 2 * GLA_V) // MEM_W, kv_mem, batch, seq)
    w_out = a_w_out[0].astype(BF16)
    h = _proj_ln(o, m, h, w_out[:GLA_V], w_out[GLA_V:], ln_g[0, 0].reshape(1, d),
                 ln_b[0, 0].reshape(1, d))
    ffn = _peer(h, peer_w_q[0], peer_subkeys[0], peer_u[0], peer_v[0])
    h = _res_ln(ffn, h, ln_g[0, 1].reshape(1, d), ln_b[0, 1].reshape(1, d))

    kv_sb = _matmul(h, sb_w_kv.astype(BF16))
    proj = _matmul(h, b_w_in[0].astype(BF16))
    o = _stick_breaking(proj, kv_sb, batch, seq)
    kv_mem = _matmul(memf, b_w_mem_kv[0].astype(BF16))
    m = _mem_attn(proj, SB_W // MEM_W, kv_mem, batch, seq)
    w_out = b_w_out[0].astype(BF16)
    h = _proj_ln(o, m, h, w_out[:SB_W], w_out[SB_W:], ln_g[1, 0].reshape(1, d),
                 ln_b[1, 0].reshape(1, d))
    ffn = _peer(h, peer_w_q[1], peer_subkeys[1], peer_u[1], peer_v[1])
    h = _res_ln(ffn, h, ln_g[1, 1].reshape(1, d), ln_b[1, 1].reshape(1, d))
    return h.reshape(batch, seq, d)
```

```python
import functools

import jax
import jax.numpy as jnp
from jax import lax
from jax.experimental import pallas as pl
from jax.experimental.pallas import tpu as pltpu

F32 = jnp.float32
BF16 = jnp.bfloat16
I32 = jnp.int32

LANES = 128
SUBLANES = 8
VMEM_LIMIT_DEFAULT = 48 * 1024 * 1024
VMEM_LIMIT_TABLE = 56 * 1024 * 1024

D_MODEL = 1024
N_MEM = 256
GLA_HEADS = 6
GLA_DK = 64
GLA_DV = 128
GLA_K = GLA_HEADS * GLA_DK
GLA_V = GLA_HEADS * GLA_DV
GLA_GATE_RANK = 16
GLA_TAU = 16.0
GLA_CHUNK = 64
SB_HEADS = 12
SB_DIM = 64
SB_W = SB_HEADS * SB_DIM
SB_BLOCK = 128
MEM_HEADS = 4
MEM_DIM = 64
MEM_W = MEM_HEADS * MEM_DIM
PEER_HEADS = 8
PEER_KEYS = 128
PEER_EXPERTS = PEER_KEYS * PEER_KEYS
PEER_TOPK = 16
PEER_QHALF = 128
PEER_PICKS = PEER_HEADS * PEER_TOPK
DEPTH = 2
DEEPNORM_ALPHA = (2.0 * DEPTH) ** 0.25
EPS = 1e-5

ROW_CHUNKS = D_MODEL // LANES
ROW_WORDS = ROW_CHUNKS // 2

NT_DIMS = (((1,), (1,)), ((), ()))
TN_DIMS = (((0,), (0,)), ((), ()))


def _params(semantics, vmem=VMEM_LIMIT_DEFAULT):
    return pltpu.CompilerParams(dimension_semantics=semantics, vmem_limit_bytes=vmem)


def _log_sigmoid(z):
    return jnp.minimum(z, 0.0) - jnp.log1p(jnp.exp(-jnp.abs(z)))


def _dot(a, b):
    return jnp.dot(a, b, preferred_element_type=F32)


def _split_bf16(x, parts):
    out = []
    r = x
    for _ in range(parts):
        p = r.astype(BF16)
        out.append(p)
        r = r - p.astype(F32)
    return out


def _layer_norm(y, g, b):
    mu = jnp.mean(y, axis=-1, keepdims=True)
    yc = y - mu
    var = jnp.mean(yc * yc, axis=-1, keepdims=True)
    return yc * lax.rsqrt(var + EPS) * g + b


def _matmul_kernel(x_ref, w_ref, o_ref):
    o_ref[...] = _dot(x_ref[...].astype(BF16), w_ref[...]).astype(o_ref.dtype)


def _matmul(x, w, out_dtype=F32, tm=512, tn=256):
    m, k = x.shape
    n = w.shape[1]
    tm = min(tm, m)
    tn = min(tn, n)
    return pl.pallas_call(
        _matmul_kernel,
        grid=(m // tm, n // tn),
        in_specs=[pl.BlockSpec((tm, k), lambda i, j: (i, 0)),
                  pl.BlockSpec((k, tn), lambda i, j: (0, j))],
        out_specs=pl.BlockSpec((tm, tn), lambda i, j: (i, j)),
        out_shape=jax.ShapeDtypeStruct((m, n), out_dtype),
        compiler_params=_params(("parallel", "arbitrary")),
        name="matmul",
    )(x, w)


GLA_ROWS = 256


def _gla_kernel(q_ref, k_ref, v_ref, r_ref, glr_ref, wg_ref, bg_ref, hg_ref, o_ref, st_ref):
    @pl.when(pl.program_id(2) == 0)
    def _():
        st_ref[...] = jnp.zeros_like(st_ref)

    c_rows = GLA_CHUNK
    lane = lax.broadcasted_iota(I32, (c_rows, LANES), 1)
    head_masks = (lane < GLA_DK, lane >= GLA_DK)
    row = lax.broadcasted_iota(I32, (c_rows, c_rows), 0)
    col = lax.broadcasted_iota(I32, (c_rows, c_rows), 1)
    causal = row >= col
    tril = causal.astype(BF16)
    wg = wg_ref[...]
    bg = bg_ref[...]
    hg = hg_ref[...]
    for c in range(GLA_ROWS // c_rows):
        rows = slice(c * c_rows, (c + 1) * c_rows)
        z = _dot(glr_ref[rows, :].astype(BF16), wg) + bg
        lg = _log_sigmoid(z) * (1.0 / GLA_TAU)
        bc = sum(_dot(tril, part) for part in _split_bf16(lg, 3))
        bl = bc[c_rows - 1:c_rows, :]
        q = q_ref[rows, :] * (GLA_DK ** -0.5)
        k = k_ref[rows, :]
        qd = q * jnp.exp(bc)
        ki = (k * jnp.exp(-bc)).astype(BF16)
        ke = k * jnp.exp(bl - bc)
        dec = jnp.exp(bl)
        for a in range(2):
            cols = slice(a * GLA_DV, (a + 1) * GLA_DV)
            qa = jnp.where(head_masks[a], qd, 0.0).astype(BF16)
            kea = jnp.where(head_masks[a], ke, 0.0).astype(BF16)
            attn = lax.dot_general(qa, ki, NT_DIMS, preferred_element_type=F32)
            attn = jnp.where(causal, attn, 0.0).astype(BF16)
            va = v_ref[rows, cols].astype(BF16)
            st = st_ref[a]
            o = _dot(attn, va) + lax.dot_general(qa, st.astype(BF16), NT_DIMS,
                                                 preferred_element_type=F32)
            st_ref[a] = st * dec + lax.dot_general(va, kea, TN_DIMS, preferred_element_type=F32)
            ms = jnp.mean(o * o, axis=-1, keepdims=True)
            r = r_ref[rows, cols]
            silu = r / (1.0 + jnp.exp(-r))
            o_ref[rows, cols] = o * lax.rsqrt(ms + EPS) * hg[:, cols] * silu


def _gla(proj, wg, bg, hg, batch, seq):
    n = proj.shape[0]
    tr = min(GLA_ROWS, seq)
    nt = seq // tr
    rowmap = lambda b, j, t: b * nt + t
    pair_w = 2 * GLA_DV
    return pl.pallas_call(
        _gla_kernel,
        grid=(batch, GLA_HEADS // 2, nt),
        in_specs=[
            pl.BlockSpec((tr, LANES), lambda b, j, t: (rowmap(b, j, t), j)),
            pl.BlockSpec((tr, LANES), lambda b, j, t: (rowmap(b, j, t), GLA_K // LANES + j)),
            pl.BlockSpec((tr, pair_w), lambda b, j, t: (rowmap(b, j, t), 2 * GLA_K // pair_w + j)),
            pl.BlockSpec((tr, pair_w),
                         lambda b, j, t: (rowmap(b, j, t), (2 * GLA_K + GLA_V) // pair_w + j)),
            pl.BlockSpec((tr, pair_w),
                         lambda b, j, t: (rowmap(b, j, t), (2 * GLA_K + 2 * GLA_V + MEM_W) // pair_w)),
            pl.BlockSpec((pair_w, LANES), lambda b, j, t: (0, j)),
            pl.BlockSpec((1, LANES), lambda b, j, t: (0, j)),
            pl.BlockSpec((1, pair_w), lambda b, j, t: (0, j)),
        ],
        out_specs=pl.BlockSpec((tr, pair_w), lambda b, j, t: (rowmap(b, j, t), j)),
        out_shape=jax.ShapeDtypeStruct((n, GLA_V), F32),
        scratch_shapes=[pltpu.VMEM((2, GLA_DV, LANES), F32)],
        compiler_params=_params(("parallel", "parallel", "arbitrary")),
        name="gla",
    )(proj, proj, proj, proj, proj, wg, bg, hg)


MEM_ROWS = 512


def _mem_attn_kernel(q_ref, kv_ref, o_ref):
    tq = q_ref.shape[0]
    lane = lax.broadcasted_iota(I32, (tq, LANES), 1)
    lo_head = lane < MEM_DIM
    for j in range(MEM_W // LANES):
        cols = slice(j * LANES, (j + 1) * LANES)
        q2 = q_ref[:, cols]
        k2 = kv_ref[:, cols].astype(BF16)
        v2 = kv_ref[:, MEM_W + j * LANES:MEM_W + (j + 1) * LANES].astype(BF16)
        outs = []
        for a in range(2):
            qa = jnp.where(lo_head if a == 0 else ~lo_head, q2, 0.0).astype(BF16)
            s = lax.dot_general(qa, k2, NT_DIMS, preferred_element_type=F32) * (MEM_DIM ** -0.5)
            e = jnp.exp(s - jnp.max(s, axis=-1, keepdims=True))
            p = e / jnp.sum(e, axis=-1, keepdims=True)
            outs.append(_dot(p.astype(BF16), v2))
        o_ref[:, cols] = jnp.where(lo_head, outs[0], outs[1])


def _mem_attn(proj, q_col_block, kv, batch, seq):
    n = proj.shape[0]
    tq = min(MEM_ROWS, seq)
    nt = seq // tq
    return pl.pallas_call(
        _mem_attn_kernel,
        grid=(batch, nt),
        in_specs=[pl.BlockSpec((tq, MEM_W), lambda b, t: (b * nt + t, q_col_block)),
                  pl.BlockSpec((N_MEM, 2 * MEM_W), lambda b, t: (b, 0))],
        out_specs=pl.BlockSpec((tq, MEM_W), lambda b, t: (b * nt + t, 0)),
        out_shape=jax.ShapeDtypeStruct((n, MEM_W), F32),
        compiler_params=_params(("parallel", "parallel")),
        name="mem_attn",
    )(proj, kv)


SB_TILE = 256


def _sb_kernel(q_ref, k_ref, v_ref, o_ref, acc0_ref, acc1_ref, run0_ref, run1_ref):
    acc_refs = (acc0_ref, acc1_ref)
    run_refs = (run0_ref, run1_ref)
    i = pl.program_id(2)
    t = q_ref.shape[0]
    row = lax.broadcasted_iota(I32, (t, t), 0)
    col = lax.broadcasted_iota(I32, (t, t), 1)
    strict = col < row
    later = (row > col).astype(BF16)
    lo_head = lax.broadcasted_iota(I32, (t, LANES), 1) < SB_DIM
    q2 = q_ref[...] * (SB_DIM ** -0.5)
    qa = (jnp.where(lo_head, q2, 0.0).astype(BF16), jnp.where(lo_head, 0.0, q2).astype(BF16))
    for ref in acc_refs + run_refs:
        ref[...] = jnp.zeros_like(ref)

    def tile(j, diag):
        start = pl.multiple_of(j * t, t)
        kj = k_ref[pl.ds(start, t), :]
        vj = v_ref[pl.ds(start, t), :]
        for a in range(2):
            z = lax.dot_general(qa[a], kj, NT_DIMS, preferred_element_type=F32)
            ls = jnp.minimum(z, 0.0) - jnp.log(1.0 + jnp.exp(-jnp.abs(z)))
            ln = ls - z
            if diag:
                ln = jnp.where(strict, ln, 0.0)
            exc = sum(_dot(part, later) for part in _split_bf16(ln, 2))
            run = run_refs[a][...]
            a_w = jnp.exp(ls + exc + jnp.concatenate([run] * (t // LANES), axis=1))
            if diag:
                a_w = jnp.where(strict, a_w, 0.0)
            acc_refs[a][...] += _dot(a_w.astype(BF16), vj)
            total = exc[:, 0:1] + ln[:, 0:1]
            run_refs[a][...] = run + jnp.broadcast_to(total, (t, LANES))

    tile(i, True)

    def body(s, carry):
        tile(i - 1 - s, False)
        return carry

    lax.fori_loop(0, i, body, 0)
    o_ref[...] = jnp.where(lo_head, acc0_ref[...], acc1_ref[...])


def _stick_breaking(qproj, kv, batch, seq):
    n = qproj.shape[0]
    t = min(SB_TILE, seq)
    nq = seq // t
    pairs = SB_W // LANES
    return pl.pallas_call(
        _sb_kernel,
        grid=(batch, pairs, nq),
        in_specs=[pl.BlockSpec((t, LANES), lambda b, j, i: (b * nq + i, j)),
                  pl.BlockSpec((seq, LANES), lambda b, j, i: (b, j)),
                  pl.BlockSpec((seq, LANES), lambda b, j, i: (b, pairs + j))],
        out_specs=pl.BlockSpec((t, LANES), lambda b, j, i: (b * nq + i, j)),
        out_shape=jax.ShapeDtypeStruct((n, SB_W), F32),
        scratch_shapes=[pltpu.VMEM((t, LANES), F32)] * 4,
        compiler_params=_params(("parallel", "parallel", "arbitrary")),
        name="stick_breaking",
    )(qproj, kv, kv)


LN_ROWS = 256


def _proj_ln_kernel(a1_ref, a2_ref, h_ref, w1_ref, w2_ref, g_ref, b_ref, o_ref):
    mix = _dot(a1_ref[...].astype(BF16), w1_ref[...]) + _dot(a2_ref[...].astype(BF16), w2_ref[...])
    o_ref[...] = _layer_norm(DEEPNORM_ALPHA * h_ref[...] + mix, g_ref[...], b_ref[...])


def _proj_ln(a1, a2, h, w1, w2, g, b):
    n = h.shape[0]
    tm = min(LN_ROWS, n)
    full = lambda arr: pl.BlockSpec(arr.shape, lambda i: (0, 0))
    rows = lambda arr: pl.BlockSpec((tm, arr.shape[1]), lambda i: (i, 0))
    return pl.pallas_call(
        _proj_ln_kernel,
        grid=(n // tm,),
        in_specs=[rows(a1), rows(a2), rows(h), full(w1), full(w2), full(g), full(b)],
        out_specs=rows(h),
        out_shape=jax.ShapeDtypeStruct(h.shape, F32),
        compiler_params=_params(("parallel",)),
        name="proj_ln",
    )(a1, a2, h, w1, w2, g, b)


def _res_ln_kernel(f_ref, h_ref, g_ref, b_ref, o_ref):
    o_ref[...] = _layer_norm(DEEPNORM_ALPHA * h_ref[...] + f_ref[...], g_ref[...], b_ref[...])


def _res_ln(f, h, g, b):
    n = h.shape[0]
    tm = min(512, n)
    full = lambda arr: pl.BlockSpec(arr.shape, lambda i: (0, 0))
    rows = lambda arr: pl.BlockSpec((tm, arr.shape[1]), lambda i: (i, 0))
    return pl.pallas_call(
        _res_ln_kernel,
        grid=(n // tm,),
        in_specs=[rows(f), rows(h), full(g), full(b)],
        out_specs=rows(h),
        out_shape=jax.ShapeDtypeStruct(h.shape, F32),
        compiler_params=_params(("parallel",)),
        name="res_ln",
    )(f, h, g, b)


TOPK_TOKENS = 128


def _top16_rows(vals, payload=None):
    k = vals.shape[0]
    iota = lax.broadcasted_iota(I32, vals.shape, 0).astype(F32)
    out_v, out_p = [], []
    for _ in range(PEER_TOPK):
        m = jnp.max(vals, axis=0, keepdims=True)
        am = jnp.min(jnp.where(vals == m, iota, float(k)), axis=0, keepdims=True)
        hit = iota == am
        out_v.append(m)
        if payload is None:
            out_p.append(am)
        else:
            out_p.append(jnp.max(jnp.where(hit, payload, -1.0), axis=0, keepdims=True))
        vals = jnp.where(hit, -jnp.inf, vals)
    return out_v, out_p


def _peer_topk_kernel(q_ref, sk_ref, idx_ref, g_ref):
    tops = []
    for p in range(2):
        qp = q_ref[:, p * PEER_QHALF:(p + 1) * PEER_QHALF].astype(BF16)
        s_t = lax.dot_general(sk_ref[0, p], qp, NT_DIMS, preferred_element_type=F32)
        tops.append(_top16_rows(s_t))
    (s0, i0), (s1, i1) = tops
    s1_all = jnp.concatenate(s1, axis=0)
    i1_all = jnp.concatenate(i1, axis=0)
    cand_s = jnp.concatenate([s0[i] + s1_all for i in range(PEER_TOPK)], axis=0)
    cand_i = jnp.concatenate([i0[i] * float(PEER_KEYS) + i1_all for i in range(PEER_TOPK)], axis=0)
    best_s, best_i = _top16_rows(cand_s, cand_i)
    e = [jnp.exp(s - best_s[0]) for s in best_s]
    denom = sum(e)
    g_ref[...] = jnp.concatenate(e, axis=0) / denom
    idx_ref[...] = jnp.concatenate(best_i, axis=0).astype(I32)


def _peer_topk(qp, subkeys):
    n = qp.shape[0]
    tt = min(TOPK_TOKENS, n)
    spec_out = pl.BlockSpec((PEER_TOPK, tt), lambda i, h: (h, i))
    return pl.pallas_call(
        _peer_topk_kernel,
        grid=(n // tt, PEER_HEADS),
        in_specs=[pl.BlockSpec((tt, 2 * PEER_QHALF), lambda i, h: (i, h)),
                  pl.BlockSpec((1, 2, PEER_KEYS, PEER_QHALF), lambda i, h: (h, 0, 0, 0))],
        out_specs=[spec_out, spec_out],
        out_shape=[jax.ShapeDtypeStruct((PEER_PICKS, n), I32),
                   jax.ShapeDtypeStruct((PEER_PICKS, n), F32)],
        compiler_params=_params(("parallel", "parallel")),
        name="peer_topk",
    )(qp, subkeys)


def _pack_kernel(x_ref, o_ref):
    o_ref[...] = pltpu.bitcast(x_ref[...].astype(BF16), I32)


def _pack_rows(x2d):
    r = x2d.shape[0]
    tr = min(4096, r)
    return pl.pallas_call(
        _pack_kernel,
        grid=(r // tr,),
        in_specs=[pl.BlockSpec((tr, LANES), lambda i: (i, 0))],
        out_specs=pl.BlockSpec((tr // 2, LANES), lambda i: (i, 0)),
        out_shape=jax.ShapeDtypeStruct((r // 2, LANES), I32),
        compiler_params=_params(("parallel",)),
        name="pack_bf16",
    )(x2d)


def _pack_table(tab):
    e = tab.shape[0]
    return _pack_rows(tab.reshape(e * ROW_CHUNKS, LANES)).reshape(e, ROW_WORDS, LANES)


PEER_TOKENS = 64
PEER_UNROLL = 2
PAIRS = PEER_PICKS // 2


def _gather_rows(idx_ref, tab_ref, n):
    pieces = []
    for k in range(PAIRS):
        e0 = idx_ref[0, n, 2 * k]
        e1 = idx_ref[0, n, 2 * k + 1]
        words = jnp.concatenate([tab_ref[e0], tab_ref[e1]], axis=0)
        pieces.append(pltpu.bitcast(words, BF16))
    return pieces


def _token_loop(n_tokens, token):
    def body(m, carry):
        for u in range(PEER_UNROLL):
            token(m * PEER_UNROLL + u)
        return carry

    lax.fori_loop(0, n_tokens // PEER_UNROLL, body, 0)


def _peer_u_kernel(idx_ref, x_ref, tab_ref, o_ref):
    ones = jnp.ones((LANES, LANES), BF16)
    lane = lax.broadcasted_iota(I32, (ROW_CHUNKS, LANES), 1)

    def token(n):
        xw = pltpu.bitcast(x_ref[n], BF16)
        prods = [piece * xw for piece in _gather_rows(idx_ref, tab_ref, n)]
        z = _dot(jnp.concatenate(prods, axis=0), ones)
        n_acc = 4
        acc = [jnp.zeros((ROW_CHUNKS, LANES), F32) for _ in range(n_acc)]
        for e in range(PEER_PICKS):
            acc[e % n_acc] = acc[e % n_acc] + jnp.where(
                lane == e, z[e * ROW_CHUNKS:(e + 1) * ROW_CHUNKS, :], 0.0)
        o_ref[n] = jnp.sum(sum(acc), axis=0, keepdims=True)

    _token_loop(o_ref.shape[0], token)


def _peer_v_kernel(idx_ref, wexp_ref, tab_ref, o_ref):
    lane = lax.broadcasted_iota(I32, (ROW_CHUNKS, LANES), 1)
    sub = lax.broadcasted_iota(I32, (ROW_CHUNKS, LANES), 0)
    own_chunk = (lane & (ROW_CHUNKS - 1)) == sub

    def token(n):
        rows = jnp.concatenate(_gather_rows(idx_ref, tab_ref, n), axis=0)
        lhs = jnp.concatenate(
            [jnp.where(own_chunk, jnp.broadcast_to(wexp_ref[j, n], (ROW_CHUNKS, LANES)), 0.0)
             for j in range(ROW_CHUNKS)], axis=1).astype(BF16)
        o_ref[n] = _dot(lhs, rows)

    _token_loop(o_ref.shape[0], token)


def _idx_spec(tb):
    return pl.BlockSpec((1, tb, PEER_PICKS), lambda i: (i, 0, 0), memory_space=pltpu.SMEM)


def _table_spec(tab):
    return pl.BlockSpec(tab.shape, lambda i: (0, 0, 0), pipeline_mode=pl.Buffered(1))


def _peer_u(idx, xw, tab):
    n = idx.shape[0]
    tb = min(PEER_TOKENS, n)
    out = pl.pallas_call(
        _peer_u_kernel,
        grid=(n // tb,),
        in_specs=[_idx_spec(tb), pl.BlockSpec((tb, SUBLANES, LANES), lambda i: (i, 0, 0)),
                  _table_spec(tab)],
        out_specs=pl.BlockSpec((tb, 1, LANES), lambda i: (i, 0, 0)),
        out_shape=jax.ShapeDtypeStruct((n, 1, LANES), F32),
        compiler_params=_params(("arbitrary",), VMEM_LIMIT_TABLE),
        name="peer_u",
    )(idx.reshape(n // tb, tb, PEER_PICKS), xw, tab)
    return out.reshape(n, LANES)


def _peer_v(idx, wexp, tab):
    n = idx.shape[0]
    tb = min(PEER_TOKENS, n)
    out = pl.pallas_call(
        _peer_v_kernel,
        grid=(n // tb,),
        in_specs=[_idx_spec(tb),
                  pl.BlockSpec((ROW_CHUNKS, tb, 1, LANES), lambda i: (0, i, 0, 0)),
                  _table_spec(tab)],
        out_specs=pl.BlockSpec((tb, ROW_CHUNKS, LANES), lambda i: (i, 0, 0)),
        out_shape=jax.ShapeDtypeStruct((n, ROW_CHUNKS, LANES), F32),
        compiler_params=_params(("arbitrary",), VMEM_LIMIT_TABLE),
        name="peer_v",
    )(idx.reshape(n // tb, tb, PEER_PICKS), wexp, tab)
    return out.reshape(n, D_MODEL)


def _gate_act_kernel(a_ref, g_ref, e_ref, o_ref):
    a = a_ref[...]
    w = g_ref[...] * (0.5 * a * (1.0 + lax.erf(a * (2.0 ** -0.5))))
    wx = _dot(w.astype(BF16), e_ref[...])
    for j in range(ROW_CHUNKS):
        o_ref[j] = wx[:, j * LANES:(j + 1) * LANES]


def _gate_act(act, g):
    n = act.shape[0]
    tm = min(1024, n)
    spec = pl.BlockSpec((tm, LANES), lambda i: (i, 0))
    expand = (jnp.arange(D_MODEL)[None, :] // ROW_CHUNKS == jnp.arange(LANES)[:, None]).astype(BF16)
    return pl.pallas_call(
        _gate_act_kernel,
        grid=(n // tm,),
        in_specs=[spec, spec, pl.BlockSpec((LANES, D_MODEL), lambda i: (0, 0))],
        out_specs=pl.BlockSpec((ROW_CHUNKS, tm, LANES), lambda i: (0, i, 0)),
        out_shape=jax.ShapeDtypeStruct((ROW_CHUNKS, n, LANES), F32),
        compiler_params=_params(("parallel",)),
        name="peer_gate_act",
    )(act, g, expand)


def _peer(h, w_q, subkeys, u_tab, v_tab):
    n = h.shape[0]
    qp = _matmul(h, w_q.astype(BF16))
    idx_t, g_t = _peer_topk(qp, subkeys.astype(BF16))
    idx = idx_t.T
    g = g_t.T
    hr = h.reshape(n, ROW_CHUNKS, LANES)
    xw = _pack_rows(jnp.concatenate([hr, hr], axis=1).reshape(n * 2 * ROW_CHUNKS, LANES))
    act = _peer_u(idx, xw.reshape(n, SUBLANES, LANES), _pack_table(u_tab))
    wexp = _gate_act(act, g).reshape(ROW_CHUNKS, n, 1, LANES)
    return _peer_v(idx, wexp, _pack_table(v_tab))


def kernel(x, mem, a_w_in, a_w_gate2, a_b_gate, a_head_g, a_w_mem_kv, a_w_out, b_w_in, b_w_mem_kv,
           b_w_out, sb_w_kv, peer_w_q, peer_subkeys, peer_u, peer_v, ln_g, ln_b):
    batch, seq, d = x.shape
    n = batch * seq
    h = x.reshape(n, d)
    memf = mem.reshape(batch * N_MEM, d)
    pad_w = 2 * GLA_DV
    s = [GLA_K, 2 * GLA_K, 2 * GLA_K + GLA_V, 2 * GLA_K + 2 * GLA_V,
         2 * GLA_K + 2 * GLA_V + GLA_GATE_RANK]

    w_in = a_w_in[0]
    w_in_r = jnp.concatenate(
        [w_in[:, :s[3]], w_in[:, s[4]:], w_in[:, s[3]:s[4]],
         jnp.zeros((d, pad_w - GLA_GATE_RANK), F32)], axis=1).astype(BF16)
    proj = _matmul(h, w_in_r)
    wg = jnp.concatenate([a_w_gate2[0], jnp.zeros((pad_w - GLA_GATE_RANK, GLA_K), F32)],
                         axis=0).astype(BF16)
    o = _gla(proj, wg, a_b_gate[0].reshape(1, GLA_K), a_head_g[0].reshape(1, GLA_V), batch, seq)
    kv_mem = _matmul(memf, a_w_mem_kv[0].astype(BF16))
    m = _mem_attn(proj, (2 * GLA_K + 2 * GLA_V) // MEM_W, kv_mem, batch, seq)
    w_out = a_w_out[0].astype(BF16)
    h = _proj_ln(o, m, h, w_out[:GLA_V], w_out[GLA_V:], ln_g[0, 0].reshape(1, d),
                 ln_b[0, 0].reshape(1, d))
    ffn = _peer(h, peer_w_q[0], peer_subkeys[0], peer_u[0], peer_v[0])
    h = _res_ln(ffn, h, ln_g[0, 1].reshape(1, d), ln_b[0, 1].reshape(1, d))

    kv_sb = _matmul(h, sb_w_kv.astype(BF16), out_dtype=BF16)
    proj = _matmul(h, b_w_in[0].astype(BF16))
    o = _stick_breaking(proj, kv_sb, batch, seq)
    kv_mem = _matmul(memf, b_w_mem_kv[0].astype(BF16))
    m = _mem_attn(proj, SB_W // MEM_W, kv_mem, batch, seq)
    w_out = b_w_out[0].astype(BF16)
    h = _proj_ln(o, m, h, w_out[:SB_W], w_out[SB_W:], ln_g[1, 0].reshape(1, d),
                 ln_b[1, 0].reshape(1, d))
    ffn = _peer(h, peer_w_q[1], peer_subkeys[1], peer_u[1], peer_v[1])
    h = _res_ln(ffn, h, ln_g[1, 1].reshape(1, d), ln_b[1, 1].reshape(1, d))
    return h.reshape(batch, seq, d)
```

```python
import functools

import jax
import jax.numpy as jnp
from jax import lax
from jax.experimental import pallas as pl
from jax.experimental.pallas import tpu as pltpu

F32 = jnp.float32
BF16 = jnp.bfloat16
I32 = jnp.int32

LANES = 128
SUBLANES = 8
VMEM_LIMIT_DEFAULT = 48 * 1024 * 1024
VMEM_LIMIT_TABLE = 56 * 1024 * 1024

D_MODEL = 1024
N_MEM = 256
GLA_HEADS = 6
GLA_DK = 64
GLA_DV = 128
GLA_K = GLA_HEADS * GLA_DK
GLA_V = GLA_HEADS * GLA_DV
GLA_GATE_RANK = 16
GLA_TAU = 16.0
GLA_CHUNK = 64
SB_HEADS = 12
SB_DIM = 64
SB_W = SB_HEADS * SB_DIM
SB_BLOCK = 128
MEM_HEADS = 4
MEM_DIM = 64
MEM_W = MEM_HEADS * MEM_DIM
PEER_HEADS = 8
PEER_KEYS = 128
PEER_EXPERTS = PEER_KEYS * PEER_KEYS
PEER_TOPK = 16
PEER_QHALF = 128
PEER_PICKS = PEER_HEADS * PEER_TOPK
DEPTH = 2
DEEPNORM_ALPHA = (2.0 * DEPTH) ** 0.25
EPS = 1e-5

ROW_CHUNKS = D_MODEL // LANES
ROW_WORDS = ROW_CHUNKS // 2

NT_DIMS = (((1,), (1,)), ((), ()))
TN_DIMS = (((0,), (0,)), ((), ()))


def _params(semantics, vmem=VMEM_LIMIT_DEFAULT):
    return pltpu.CompilerParams(dimension_semantics=semantics, vmem_limit_bytes=vmem)


def _log_sigmoid(z):
    return jnp.minimum(z, 0.0) - jnp.log1p(jnp.exp(-jnp.abs(z)))


def _dot(a, b):
    return jnp.dot(a, b, preferred_element_type=F32)


def _split_bf16(x, parts):
    out = []
    r = x
    for _ in range(parts):
        p = r.astype(BF16)
        out.append(p)
        r = r - p.astype(F32)
    return out


def _layer_norm(y, g, b):
    mu = jnp.mean(y, axis=-1, keepdims=True)
    yc = y - mu
    var = jnp.mean(yc * yc, axis=-1, keepdims=True)
    return yc * lax.rsqrt(var + EPS) * g + b


def _matmul_kernel(x_ref, w_ref, o_ref):
    o_ref[...] = _dot(x_ref[...].astype(BF16), w_ref[...]).astype(o_ref.dtype)


def _matmul(x, w, out_dtype=F32, tm=512, tn=256):
    m, k = x.shape
    n = w.shape[1]
    tm = min(tm, m)
    tn = min(tn, n)
    return pl.pallas_call(
        _matmul_kernel,
        grid=(m // tm, n // tn),
        in_specs=[pl.BlockSpec((tm, k), lambda i, j: (i, 0)),
                  pl.BlockSpec((k, tn), lambda i, j: (0, j))],
        out_specs=pl.BlockSpec((tm, tn), lambda i, j: (i, j)),
        out_shape=jax.ShapeDtypeStruct((m, n), out_dtype),
        compiler_params=_params(("parallel", "arbitrary")),
        name="matmul",
    )(x, w)


GLA_ROWS = 256


def _gla_kernel(q_ref, k_ref, v_ref, r_ref, glr_ref, wg_ref, bg_ref, hg_ref, o_ref, st_ref):
    @pl.when(pl.program_id(2) == 0)
    def _():
        st_ref[...] = jnp.zeros_like(st_ref)

    c_rows = GLA_CHUNK
    lane = lax.broadcasted_iota(I32, (c_rows, LANES), 1)
    head_masks = (lane < GLA_DK, lane >= GLA_DK)
    row = lax.broadcasted_iota(I32, (c_rows, c_rows), 0)
    col = lax.broadcasted_iota(I32, (c_rows, c_rows), 1)
    causal = row >= col
    tril = causal.astype(BF16)
    wg = wg_ref[...]
    bg = bg_ref[...]
    hg = hg_ref[...]
    for c in range(GLA_ROWS // c_rows):
        rows = slice(c * c_rows, (c + 1) * c_rows)
        z = _dot(glr_ref[rows, :].astype(BF16), wg) + bg
        lg = _log_sigmoid(z) * (1.0 / GLA_TAU)
        bc = sum(_dot(tril, part) for part in _split_bf16(lg, 3))
        bl = bc[c_rows - 1:c_rows, :]
        q = q_ref[rows, :] * (GLA_DK ** -0.5)
        k = k_ref[rows, :]
        qd = q * jnp.exp(bc)
        ki = (k * jnp.exp(-bc)).astype(BF16)
        ke = k * jnp.exp(bl - bc)
        dec = jnp.exp(bl)
        for a in range(2):
            cols = slice(a * GLA_DV, (a + 1) * GLA_DV)
            qa = jnp.where(head_masks[a], qd, 0.0).astype(BF16)
            kea = jnp.where(head_masks[a], ke, 0.0).astype(BF16)
            attn = lax.dot_general(qa, ki, NT_DIMS, preferred_element_type=F32)
            attn = jnp.where(causal, attn, 0.0).astype(BF16)
            va = v_ref[rows, cols].astype(BF16)
            st = st_ref[a]
            o = _dot(attn, va) + lax.dot_general(qa, st.astype(BF16), NT_DIMS,
                                                 preferred_element_type=F32)
            st_ref[a] = st * dec + lax.dot_general(va, kea, TN_DIMS, preferred_element_type=F32)
            ms = jnp.mean(o * o, axis=-1, keepdims=True)
            r = r_ref[rows, cols]
            silu = r / (1.0 + jnp.exp(-r))
            o_ref[rows, cols] = o * lax.rsqrt(ms + EPS) * hg[:, cols] * silu


def _gla(proj, wg, bg, hg, batch, seq):
    n = proj.shape[0]
    tr = min(GLA_ROWS, seq)
    nt = seq // tr
    rowmap = lambda b, j, t: b * nt + t
    pair_w = 2 * GLA_DV
    return pl.pallas_call(
        _gla_kernel,
        grid=(batch, GLA_HEADS // 2, nt),
        in_specs=[
            pl.BlockSpec((tr, LANES), lambda b, j, t: (rowmap(b, j, t), j)),
            pl.BlockSpec((tr, LANES), lambda b, j, t: (rowmap(b, j, t), GLA_K // LANES + j)),
            pl.BlockSpec((tr, pair_w), lambda b, j, t: (rowmap(b, j, t), 2 * GLA_K // pair_w + j)),
            pl.BlockSpec((tr, pair_w),
                         lambda b, j, t: (rowmap(b, j, t), (2 * GLA_K + GLA_V) // pair_w + j)),
            pl.BlockSpec((tr, pair_w),
                         lambda b, j, t: (rowmap(b, j, t), (2 * GLA_K + 2 * GLA_V + MEM_W) // pair_w)),
            pl.BlockSpec((pair_w, LANES), lambda b, j, t: (0, j)),
            pl.BlockSpec((1, LANES), lambda b, j, t: (0, j)),
            pl.BlockSpec((1, pair_w), lambda b, j, t: (0, j)),
        ],
        out_specs=pl.BlockSpec((tr, pair_w), lambda b, j, t: (rowmap(b, j, t), j)),
        out_shape=jax.ShapeDtypeStruct((n, GLA_V), F32),
        scratch_shapes=[pltpu.VMEM((2, GLA_DV, LANES), F32)],
        compiler_params=_params(("parallel", "parallel", "arbitrary")),
        name="gla",
    )(proj, proj, proj, proj, proj, wg, bg, hg)


MEM_ROWS = 512


def _mem_attn_kernel(q_ref, kv_ref, o_ref):
    tq = q_ref.shape[0]
    lane = lax.broadcasted_iota(I32, (tq, LANES), 1)
    lo_head = lane < MEM_DIM
    for j in range(MEM_W // LANES):
        cols = slice(j * LANES, (j + 1) * LANES)
        q2 = q_ref[:, cols]
        k2 = kv_ref[:, cols].astype(BF16)
        v2 = kv_ref[:, MEM_W + j * LANES:MEM_W + (j + 1) * LANES].astype(BF16)
        outs = []
        for a in range(2):
            qa = jnp.where(lo_head if a == 0 else ~lo_head, q2, 0.0).astype(BF16)
            s = lax.dot_general(qa, k2, NT_DIMS, preferred_element_type=F32) * (MEM_DIM ** -0.5)
            e = jnp.exp(s - jnp.max(s, axis=-1, keepdims=True))
            p = e / jnp.sum(e, axis=-1, keepdims=True)
            outs.append(_dot(p.astype(BF16), v2))
        o_ref[:, cols] = jnp.where(lo_head, outs[0], outs[1])


def _mem_attn(proj, q_col_block, kv, batch, seq):
    n = proj.shape[0]
    tq = min(MEM_ROWS, seq)
    nt = seq // tq
    return pl.pallas_call(
        _mem_attn_kernel,
        grid=(batch, nt),
        in_specs=[pl.BlockSpec((tq, MEM_W), lambda b, t: (b * nt + t, q_col_block)),
                  pl.BlockSpec((N_MEM, 2 * MEM_W), lambda b, t: (b, 0))],
        out_specs=pl.BlockSpec((tq, MEM_W), lambda b, t: (b * nt + t, 0)),
        out_shape=jax.ShapeDtypeStruct((n, MEM_W), F32),
        compiler_params=_params(("parallel", "parallel")),
        name="mem_attn",
    )(proj, kv)


SB_TILE = 256


def _sb_kernel(q_ref, k_ref, v_ref, o_ref, acc0_ref, acc1_ref, run0_ref, run1_ref):
    acc_refs = (acc0_ref, acc1_ref)
    run_refs = (run0_ref, run1_ref)
    i = pl.program_id(2)
    t = q_ref.shape[0]
    row = lax.broadcasted_iota(I32, (t, t), 0)
    col = lax.broadcasted_iota(I32, (t, t), 1)
    strict = col < row
    later = (row > col).astype(BF16)
    lo_head = lax.broadcasted_iota(I32, (t, LANES), 1) < SB_DIM
    q2 = q_ref[...] * (SB_DIM ** -0.5)
    qa = (jnp.where(lo_head, q2, 0.0).astype(BF16), jnp.where(lo_head, 0.0, q2).astype(BF16))
    for ref in acc_refs + run_refs:
        ref[...] = jnp.zeros_like(ref)

    def tile(j, diag):
        start = pl.multiple_of(j * t, t)
        kj = k_ref[pl.ds(start, t), :]
        vj = v_ref[pl.ds(start, t), :]
        heads = range(2)
        z = [lax.dot_general(qa[a], kj, NT_DIMS, preferred_element_type=F32) for a in heads]
        ls = [jnp.minimum(z[a], 0.0) - jnp.log(1.0 + jnp.exp(-jnp.abs(z[a]))) for a in heads]
        ln = [ls[a] - z[a] for a in heads]
        if diag:
            ln = [jnp.where(strict, ln[a], 0.0) for a in heads]
        exc = [sum(_dot(part, later) for part in _split_bf16(ln[a], 2)) for a in heads]
        run = [run_refs[a][...] for a in heads]
        a_w = [jnp.exp(ls[a] + exc[a] + jnp.concatenate([run[a]] * (t // LANES), axis=1))
               for a in heads]
        if diag:
            a_w = [jnp.where(strict, a_w[a], 0.0) for a in heads]
        out = [_dot(a_w[a].astype(BF16), vj) for a in heads]
        for a in heads:
            acc_refs[a][...] += out[a]
            total = exc[a][:, 0:1] + ln[a][:, 0:1]
            run_refs[a][...] = run[a] + jnp.broadcast_to(total, (t, LANES))

    tile(i, True)

    def body(s, carry):
        tile(i - 1 - s, False)
        return carry

    lax.fori_loop(0, i, body, 0)
    o_ref[...] = jnp.where(lo_head, acc0_ref[...], acc1_ref[...])


def _stick_breaking(qproj, kv, batch, seq):
    n = qproj.shape[0]
    t = min(SB_TILE, seq)
    nq = seq // t
    pairs = SB_W // LANES
    return pl.pallas_call(
        _sb_kernel,
        grid=(batch, pairs, nq),
        in_specs=[pl.BlockSpec((t, LANES), lambda b, j, i: (b * nq + i, j)),
                  pl.BlockSpec((seq, LANES), lambda b, j, i: (b, j)),
                  pl.BlockSpec((seq, LANES), lambda b, j, i: (b, pairs + j))],
        out_specs=pl.BlockSpec((t, LANES), lambda b, j, i: (b * nq + i, j)),
        out_shape=jax.ShapeDtypeStruct((n, SB_W), F32),
        scratch_shapes=[pltpu.VMEM((t, LANES), F32)] * 4,
        compiler_params=_params(("parallel", "parallel", "arbitrary")),
        name="stick_breaking",
    )(qproj, kv, kv)


LN_ROWS = 256


def _proj_ln_kernel(a1_ref, a2_ref, h_ref, w1_ref, w2_ref, g_ref, b_ref, o_ref):
    mix = _dot(a1_ref[...].astype(BF16), w1_ref[...]) + _dot(a2_ref[...].astype(BF16), w2_ref[...])
    o_ref[...] = _layer_norm(DEEPNORM_ALPHA * h_ref[...] + mix, g_ref[...], b_ref[...])


def _proj_ln(a1, a2, h, w1, w2, g, b):
    n = h.shape[0]
    tm = min(LN_ROWS, n)
    full = lambda arr: pl.BlockSpec(arr.shape, lambda i: (0, 0))
    rows = lambda arr: pl.BlockSpec((tm, arr.shape[1]), lambda i: (i, 0))
    return pl.pallas_call(
        _proj_ln_kernel,
        grid=(n // tm,),
        in_specs=[rows(a1), rows(a2), rows(h), full(w1), full(w2), full(g), full(b)],
        out_specs=rows(h),
        out_shape=jax.ShapeDtypeStruct(h.shape, F32),
        compiler_params=_params(("parallel",)),
        name="proj_ln",
    )(a1, a2, h, w1, w2, g, b)


def _res_ln_kernel(f_ref, h_ref, g_ref, b_ref, o_ref):
    o_ref[...] = _layer_norm(DEEPNORM_ALPHA * h_ref[...] + f_ref[...], g_ref[...], b_ref[...])


def _res_ln(f, h, g, b):
    n = h.shape[0]
    tm = min(512, n)
    full = lambda arr: pl.BlockSpec(arr.shape, lambda i: (0, 0))
    rows = lambda arr: pl.BlockSpec((tm, arr.shape[1]), lambda i: (i, 0))
    return pl.pallas_call(
        _res_ln_kernel,
        grid=(n // tm,),
        in_specs=[rows(f), rows(h), full(g), full(b)],
        out_specs=rows(h),
        out_shape=jax.ShapeDtypeStruct(h.shape, F32),
        compiler_params=_params(("parallel",)),
        name="res_ln",
    )(f, h, g, b)


TOPK_TOKENS = 256


def _top16_rows(problems):
    vals = [v for v, _ in problems]
    iotas = [lax.broadcasted_iota(I32, v.shape, 0).astype(F32) for v in vals]
    outs = [([], []) for _ in problems]
    for _ in range(PEER_TOPK):
        for p, (_, payload) in enumerate(problems):
            m = jnp.max(vals[p], axis=0, keepdims=True)
            am = jnp.min(jnp.where(vals[p] == m, iotas[p], float(vals[p].shape[0])),
                         axis=0, keepdims=True)
            hit = iotas[p] == am
            outs[p][0].append(m)
            if payload is None:
                outs[p][1].append(am)
            else:
                outs[p][1].append(jnp.max(jnp.where(hit, payload, -1.0), axis=0, keepdims=True))
            vals[p] = jnp.where(hit, -jnp.inf, vals[p])
    return outs


def _pair_candidates(top0, top1):
    (s0, i0), (s1, i1) = top0, top1
    s0_all = jnp.concatenate(s0, axis=0)
    i0_all = jnp.concatenate(i0, axis=0)
    s1_all = jnp.concatenate(s1, axis=0)
    i1_all = jnp.concatenate(i1, axis=0)
    half = PEER_TOPK // 2
    sub = lax.broadcasted_iota(I32, (half, s0_all.shape[1]), 0)
    cand_s = [s0[0] + s1_all]
    cand_i = [i0[0] * float(PEER_KEYS) + i1_all]
    for i in range(1, half):
        cand_s.append(jnp.where(sub < PEER_TOPK // (i + 1), s0[i] + s1_all[:half], -jnp.inf))
        cand_i.append(i0[i] * float(PEER_KEYS) + i1_all[:half])
    cand_s.append(s0_all[half:] + s1[0])
    cand_i.append(i0_all[half:] * float(PEER_KEYS) + i1[0])
    return jnp.concatenate(cand_s, axis=0), jnp.concatenate(cand_i, axis=0)


def _peer_topk_kernel(q_ref, sk_ref, idx_ref, g_ref):
    groups = q_ref.shape[0] // LANES
    scores = []
    for grp in range(groups):
        for p in range(2):
            qp = q_ref[grp * LANES:(grp + 1) * LANES, p * PEER_QHALF:(p + 1) * PEER_QHALF]
            s_t = lax.dot_general(sk_ref[0, p], qp.astype(BF16), NT_DIMS, preferred_element_type=F32)
            scores.append((s_t, None))
    tops = _top16_rows(scores)
    best = _top16_rows([_pair_candidates(tops[2 * grp], tops[2 * grp + 1]) for grp in range(groups)])
    for grp, (best_s, best_i) in enumerate(best):
        cols = slice(grp * LANES, (grp + 1) * LANES)
        e = [jnp.exp(s - best_s[0]) for s in best_s]
        g_ref[:, cols] = jnp.concatenate(e, axis=0) / sum(e)
        idx_ref[:, cols] = (jnp.concatenate(best_i, axis=0) * float(ROW_WORDS)).astype(I32)


def _peer_topk(qp, subkeys):
    n = qp.shape[0]
    tt = min(TOPK_TOKENS, n)
    spec_out = pl.BlockSpec((PEER_TOPK, tt), lambda i, h: (h, i))
    return pl.pallas_call(
        _peer_topk_kernel,
        grid=(n // tt, PEER_HEADS),
        in_specs=[pl.BlockSpec((tt, 2 * PEER_QHALF), lambda i, h: (i, h)),
                  pl.BlockSpec((1, 2, PEER_KEYS, PEER_QHALF), lambda i, h: (h, 0, 0, 0))],
        out_specs=[spec_out, spec_out],
        out_shape=[jax.ShapeDtypeStruct((PEER_PICKS, n), I32),
                   jax.ShapeDtypeStruct((PEER_PICKS, n), F32)],
        compiler_params=_params(("parallel", "parallel")),
        name="peer_topk",
    )(qp, subkeys)


def _pack_kernel(x_ref, o_ref):
    o_ref[...] = pltpu.bitcast(x_ref[...].astype(BF16), I32)


def _pack_rows(x2d):
    r = x2d.shape[0]
    tr = min(4096, r)
    return pl.pallas_call(
        _pack_kernel,
        grid=(r // tr,),
        in_specs=[pl.BlockSpec((tr, LANES), lambda i: (i, 0))],
        out_specs=pl.BlockSpec((tr // 2, LANES), lambda i: (i, 0)),
        out_shape=jax.ShapeDtypeStruct((r // 2, LANES), I32),
        compiler_params=_params(("parallel",)),
        name="pack_bf16",
    )(x2d)


def _pack_table(tab):
    e = tab.shape[0]
    return _pack_rows(tab.reshape(e * ROW_CHUNKS, LANES))


PEER_TOKENS = 64
PEER_UNROLL = 8
PAIRS = PEER_PICKS // 2


def _gather_rows(idx_ref, tab_ref, n):
    pieces = []
    for k in range(PAIRS):
        e0 = idx_ref[0, n, 2 * k]
        e1 = idx_ref[0, n, 2 * k + 1]
        words = jnp.concatenate(
            [tab_ref[pl.ds(pl.multiple_of(e, ROW_WORDS), ROW_WORDS), :] for e in (e0, e1)],
            axis=0)
        pieces.append(pltpu.bitcast(words, BF16))
    return pieces


def _token_loop(n_tokens, token):
    def body(m, carry):
        for u in range(PEER_UNROLL):
            token(m * PEER_UNROLL + u)
        return carry

    lax.fori_loop(0, n_tokens // PEER_UNROLL, body, 0)


def _peer_u_kernel(idx_ref, x_ref, tab_ref, o_ref):
    ones = jnp.ones((LANES, LANES), BF16)
    lane = lax.broadcasted_iota(I32, (ROW_CHUNKS, LANES), 1)

    def token(n):
        xw = pltpu.bitcast(x_ref[n], BF16)
        prods = [piece * xw for piece in _gather_rows(idx_ref, tab_ref, n)]
        z = _dot(jnp.concatenate(prods, axis=0), ones)
        n_acc = 4
        acc = [jnp.zeros((ROW_CHUNKS, LANES), F32) for _ in range(n_acc)]
        for e in range(PEER_PICKS):
            acc[e % n_acc] = acc[e % n_acc] + jnp.where(
                lane == e, z[e * ROW_CHUNKS:(e + 1) * ROW_CHUNKS, :], 0.0)
        o_ref[n] = jnp.sum(sum(acc), axis=0, keepdims=True)

    _token_loop(o_ref.shape[0], token)


def _peer_v_kernel(idx_ref, wexp_ref, tab_ref, o_ref):
    lane = lax.broadcasted_iota(I32, (ROW_CHUNKS, LANES), 1)
    sub = lax.broadcasted_iota(I32, (ROW_CHUNKS, LANES), 0)
    own_chunk = (lane & (ROW_CHUNKS - 1)) == sub

    def token(n):
        rows = jnp.concatenate(_gather_rows(idx_ref, tab_ref, n), axis=0)
        lhs = jnp.concatenate(
            [jnp.where(own_chunk, jnp.broadcast_to(wexp_ref[j, n], (ROW_CHUNKS, LANES)), 0.0)
             for j in range(ROW_CHUNKS)], axis=1).astype(BF16)
        o_ref[n] = _dot(lhs, rows)

    _token_loop(o_ref.shape[0], token)


def _idx_spec(tb):
    return pl.BlockSpec((1, tb, PEER_PICKS), lambda i: (i, 0, 0), memory_space=pltpu.SMEM)


def _table_spec(tab):
    return pl.BlockSpec(tab.shape, lambda i: (0, 0), pipeline_mode=pl.Buffered(1))


def _peer_u(idx, xw, tab):
    n = idx.shape[0]
    tb = min(PEER_TOKENS, n)
    out = pl.pallas_call(
        _peer_u_kernel,
        grid=(n // tb,),
        in_specs=[_idx_spec(tb), pl.BlockSpec((tb, SUBLANES, LANES), lambda i: (i, 0, 0)),
                  _table_spec(tab)],
        out_specs=pl.BlockSpec((tb, 1, LANES), lambda i: (i, 0, 0)),
        out_shape=jax.ShapeDtypeStruct((n, 1, LANES), F32),
        compiler_params=_params(("arbitrary",), VMEM_LIMIT_TABLE),
        name="peer_u",
    )(idx.reshape(n // tb, tb, PEER_PICKS), xw, tab)
    return out.reshape(n, LANES)


def _peer_v(idx, wexp, tab):
    n = idx.shape[0]
    tb = min(PEER_TOKENS, n)
    out = pl.pallas_call(
        _peer_v_kernel,
        grid=(n // tb,),
        in_specs=[_idx_spec(tb),
                  pl.BlockSpec((ROW_CHUNKS, tb, 1, LANES), lambda i: (0, i, 0, 0)),
                  _table_spec(tab)],
        out_specs=pl.BlockSpec((tb, ROW_CHUNKS, LANES), lambda i: (i, 0, 0)),
        out_shape=jax.ShapeDtypeStruct((n, ROW_CHUNKS, LANES), F32),
        compiler_params=_params(("arbitrary",), VMEM_LIMIT_TABLE),
        name="peer_v",
    )(idx.reshape(n // tb, tb, PEER_PICKS), wexp, tab)
    return out.reshape(n, D_MODEL)


def _gate_act_kernel(a_ref, g_ref, e_ref, o_ref):
    a = a_ref[...]
    w = g_ref[...] * (0.5 * a * (1.0 + lax.erf(a * (2.0 ** -0.5))))
    wx = _dot(w.astype(BF16), e_ref[...])
    for j in range(ROW_CHUNKS):
        o_ref[j] = wx[:, j * LANES:(j + 1) * LANES]


def _gate_act(act, g):
    n = act.shape[0]
    tm = min(1024, n)
    spec = pl.BlockSpec((tm, LANES), lambda i: (i, 0))
    expand = (jnp.arange(D_MODEL)[None, :] // ROW_CHUNKS == jnp.arange(LANES)[:, None]).astype(BF16)
    return pl.pallas_call(
        _gate_act_kernel,
        grid=(n // tm,),
        in_specs=[spec, spec, pl.BlockSpec((LANES, D_MODEL), lambda i: (0, 0))],
        out_specs=pl.BlockSpec((ROW_CHUNKS, tm, LANES), lambda i: (0, i, 0)),
        out_shape=jax.ShapeDtypeStruct((ROW_CHUNKS, n, LANES), F32),
        compiler_params=_params(("parallel",)),
        name="peer_gate_act",
    )(act, g, expand)


def _peer(h, w_q, subkeys, u_tab, v_tab):
    n = h.shape[0]
    qp = _matmul(h, w_q.astype(BF16))
    idx_t, g_t = _peer_topk(qp, subkeys.astype(BF16))
    idx = idx_t.T
    g = g_t.T
    hr = h.reshape(n, ROW_CHUNKS, LANES)
    xw = _pack_rows(jnp.concatenate([hr, hr], axis=1).reshape(n * 2 * ROW_CHUNKS, LANES))
    act = _peer_u(idx, xw.reshape(n, SUBLANES, LANES), _pack_table(u_tab))
    wexp = _gate_act(act, g).reshape(ROW_CHUNKS, n, 1, LANES)
    return _peer_v(idx, wexp, _pack_table(v_tab))


def kernel(x, mem, a_w_in, a_w_gate2, a_b_gate, a_head_g, a_w_mem_kv, a_w_out, b_w_in, b_w_mem_kv,
           b_w_out, sb_w_kv, peer_w_q, peer_subkeys, peer_u, peer_v, ln_g, ln_b):
    batch, seq, d = x.shape
    n = batch * seq
    h = x.reshape(n, d)
    memf = mem.reshape(batch * N_MEM, d)
    pad_w = 2 * GLA_DV
    s = [GLA_K, 2 * GLA_K, 2 * GLA_K + GLA_V, 2 * GLA_K + 2 * GLA_V,
         2 * GLA_K + 2 * GLA_V + GLA_GATE_RANK]

    w_in = a_w_in[0]
    w_in_r = jnp.concatenate(
        [w_in[:, :s[3]], w_in[:, s[4]:], w_in[:, s[3]:s[4]],
         jnp.zeros((d, pad_w - GLA_GATE_RANK), F32)], axis=1).astype(BF16)
    proj = _matmul(h, w_in_r)
    wg = jnp.concatenate([a_w_gate2[0], jnp.zeros((pad_w - GLA_GATE_RANK, GLA_K), F32)],
                         axis=0).astype(BF16)
    o = _gla(proj, wg, a_b_gate[0].reshape(1, GLA_K), a_head_g[0].reshape(1, GLA_V), batch, seq)
    kv_mem = _matmul(memf, a_w_mem_kv[0].astype(BF16))
    m = _mem_attn(proj, (2 * GLA_K + 2 * GLA_V) // MEM_W, kv_mem, batch, seq)
    w_out = a_w_out[0].astype(BF16)
    h = _proj_ln(o, m, h, w_out[:GLA_V], w_out[GLA_V:], ln_g[0, 0].reshape(1, d),
                 ln_b[0, 0].reshape(1, d))
    ffn = _peer(h, peer_w_q[0], peer_subkeys[0], peer_u[0], peer_v[0])
    h = _res_ln(ffn, h, ln_g[0, 1].reshape(1, d), ln_b[0, 1].reshape(1, d))

    kv_sb = _matmul(h, sb_w_kv.astype(BF16), out_dtype=BF16)
    proj = _matmul(h, b_w_in[0].astype(BF16))
    o = _stick_breaking(proj, kv_sb, batch, seq)
    kv_mem = _matmul(memf, b_w_mem_kv[0].astype(BF16))
    m = _mem_attn(proj, SB_W // MEM_W, kv_mem, batch, seq)
    w_out = b_w_out[0].astype(BF16)
    h = _proj_ln(o, m, h, w_out[:SB_W], w_out[SB_W:], ln_g[1, 0].reshape(1, d),
                 ln_b[1, 0].reshape(1, d))
    ffn = _peer(h, peer_w_q[1], peer_subkeys[1], peer_u[1], peer_v[1])
    h = _res_ln(ffn, h, ln_g[1, 1].reshape(1, d), ln_b[1, 1].reshape(1, d))
    return h.reshape(batch, seq, d)
```

```python
import functools

import jax
import jax.numpy as jnp
from jax import lax
from jax.experimental import pallas as pl
from jax.experimental.pallas import tpu as pltpu

F32 = jnp.float32
BF16 = jnp.bfloat16
I32 = jnp.int32

LANES = 128
SUBLANES = 8
VMEM_LIMIT_DEFAULT = 48 * 1024 * 1024
VMEM_LIMIT_TABLE = 56 * 1024 * 1024

D_MODEL = 1024
N_MEM = 256
GLA_HEADS = 6
GLA_DK = 64
GLA_DV = 128
GLA_K = GLA_HEADS * GLA_DK
GLA_V = GLA_HEADS * GLA_DV
GLA_GATE_RANK = 16
GLA_TAU = 16.0
GLA_CHUNK = 64
SB_HEADS = 12
SB_DIM = 64
SB_W = SB_HEADS * SB_DIM
SB_BLOCK = 128
MEM_HEADS = 4
MEM_DIM = 64
MEM_W = MEM_HEADS * MEM_DIM
PEER_HEADS = 8
PEER_KEYS = 128
PEER_EXPERTS = PEER_KEYS * PEER_KEYS
PEER_TOPK = 16
PEER_QHALF = 128
PEER_PICKS = PEER_HEADS * PEER_TOPK
DEPTH = 2
DEEPNORM_ALPHA = (2.0 * DEPTH) ** 0.25
EPS = 1e-5

ROW_CHUNKS = D_MODEL // LANES
ROW_WORDS = ROW_CHUNKS // 2

NT_DIMS = (((1,), (1,)), ((), ()))
TN_DIMS = (((0,), (0,)), ((), ()))


def _params(semantics, vmem=VMEM_LIMIT_DEFAULT):
    return pltpu.CompilerParams(dimension_semantics=semantics, vmem_limit_bytes=vmem)


def _log_sigmoid(z):
    return jnp.minimum(z, 0.0) - jnp.log1p(jnp.exp(-jnp.abs(z)))


def _dot(a, b):
    return jnp.dot(a, b, preferred_element_type=F32)


def _split_bf16(x, parts):
    out = []
    r = x
    for _ in range(parts):
        p = r.astype(BF16)
        out.append(p)
        r = r - p.astype(F32)
    return out


def _layer_norm(y, g, b):
    mu = jnp.mean(y, axis=-1, keepdims=True)
    yc = y - mu
    var = jnp.mean(yc * yc, axis=-1, keepdims=True)
    return yc * lax.rsqrt(var + EPS) * g + b


def _matmul_kernel(x_ref, w_ref, o_ref):
    o_ref[...] = _dot(x_ref[...].astype(BF16), w_ref[...]).astype(o_ref.dtype)


MATMUL_ROWS = 512


def _matmul(x, w, out_dtype=F32):
    m, k = x.shape
    n = w.shape[1]
    tm = min(MATMUL_ROWS, m)
    return pl.pallas_call(
        _matmul_kernel,
        grid=(m // tm,),
        in_specs=[pl.BlockSpec((tm, k), lambda i: (i, 0)),
                  pl.BlockSpec((k, n), lambda i: (0, 0))],
        out_specs=pl.BlockSpec((tm, n), lambda i: (i, 0)),
        out_shape=jax.ShapeDtypeStruct((m, n), out_dtype),
        compiler_params=_params(("parallel",)),
        name="matmul",
    )(x, w)


GLA_ROWS = 256


def _gla_kernel(q_ref, k_ref, v_ref, r_ref, glr_ref, wg_ref, bg_ref, hg_ref, o_ref, st_ref):
    @pl.when(pl.program_id(2) == 0)
    def _():
        st_ref[...] = jnp.zeros_like(st_ref)

    c_rows = GLA_CHUNK
    lane = lax.broadcasted_iota(I32, (c_rows, LANES), 1)
    head_masks = (lane < GLA_DK, lane >= GLA_DK)
    row = lax.broadcasted_iota(I32, (c_rows, c_rows), 0)
    col = lax.broadcasted_iota(I32, (c_rows, c_rows), 1)
    causal = row >= col
    tril = causal.astype(BF16)
    wg = wg_ref[...]
    bg = bg_ref[...]
    hg = hg_ref[...]
    for c in range(GLA_ROWS // c_rows):
        rows = slice(c * c_rows, (c + 1) * c_rows)
        z = _dot(glr_ref[rows, :].astype(BF16), wg) + bg
        lg = _log_sigmoid(z) * (1.0 / GLA_TAU)
        bc = sum(_dot(tril, part) for part in _split_bf16(lg, 3))
        bl = bc[c_rows - 1:c_rows, :]
        q = q_ref[rows, :] * (GLA_DK ** -0.5)
        k = k_ref[rows, :]
        qd = q * jnp.exp(bc)
        ki = (k * jnp.exp(-bc)).astype(BF16)
        ke = k * jnp.exp(bl - bc)
        dec = jnp.exp(bl)
        for a in range(2):
            cols = slice(a * GLA_DV, (a + 1) * GLA_DV)
            qa = jnp.where(head_masks[a], qd, 0.0).astype(BF16)
            kea = jnp.where(head_masks[a], ke, 0.0).astype(BF16)
            attn = lax.dot_general(qa, ki, NT_DIMS, preferred_element_type=F32)
            attn = jnp.where(causal, attn, 0.0).astype(BF16)
            va = v_ref[rows, cols].astype(BF16)
            st = st_ref[a]
            o = _dot(attn, va) + lax.dot_general(qa, st.astype(BF16), NT_DIMS,
                                                 preferred_element_type=F32)
            st_ref[a] = st * dec + lax.dot_general(va, kea, TN_DIMS, preferred_element_type=F32)
            ms = jnp.mean(o * o, axis=-1, keepdims=True)
            r = r_ref[rows, cols]
            silu = r / (1.0 + jnp.exp(-r))
            o_ref[rows, cols] = o * lax.rsqrt(ms + EPS) * hg[:, cols] * silu


def _gla(proj, wg, bg, hg, batch, seq):
    n = proj.shape[0]
    tr = min(GLA_ROWS, seq)
    nt = seq // tr
    rowmap = lambda b, j, t: b * nt + t
    pair_w = 2 * GLA_DV
    return pl.pallas_call(
        _gla_kernel,
        grid=(batch, GLA_HEADS // 2, nt),
        in_specs=[
            pl.BlockSpec((tr, LANES), lambda b, j, t: (rowmap(b, j, t), j)),
            pl.BlockSpec((tr, LANES), lambda b, j, t: (rowmap(b, j, t), GLA_K // LANES + j)),
            pl.BlockSpec((tr, pair_w), lambda b, j, t: (rowmap(b, j, t), 2 * GLA_K // pair_w + j)),
            pl.BlockSpec((tr, pair_w),
                         lambda b, j, t: (rowmap(b, j, t), (2 * GLA_K + GLA_V) // pair_w + j)),
            pl.BlockSpec((tr, pair_w),
                         lambda b, j, t: (rowmap(b, j, t), (2 * GLA_K + 2 * GLA_V + MEM_W) // pair_w)),
            pl.BlockSpec((pair_w, LANES), lambda b, j, t: (0, j)),
            pl.BlockSpec((1, LANES), lambda b, j, t: (0, j)),
            pl.BlockSpec((1, pair_w), lambda b, j, t: (0, j)),
        ],
        out_specs=pl.BlockSpec((tr, pair_w), lambda b, j, t: (rowmap(b, j, t), j)),
        out_shape=jax.ShapeDtypeStruct((n, GLA_V), F32),
        scratch_shapes=[pltpu.VMEM((2, GLA_DV, LANES), F32)],
        compiler_params=_params(("parallel", "parallel", "arbitrary")),
        name="gla",
    )(proj, proj, proj, proj, proj, wg, bg, hg)


MEM_ROWS = 512


def _mem_attn_kernel(q_ref, kv_ref, o_ref):
    tq = q_ref.shape[0]
    lane = lax.broadcasted_iota(I32, (tq, LANES), 1)
    lo_head = lane < MEM_DIM
    for j in range(MEM_W // LANES):
        cols = slice(j * LANES, (j + 1) * LANES)
        q2 = q_ref[:, cols]
        k2 = kv_ref[:, cols].astype(BF16)
        v2 = kv_ref[:, MEM_W + j * LANES:MEM_W + (j + 1) * LANES].astype(BF16)
        outs = []
        for a in range(2):
            qa = jnp.where(lo_head if a == 0 else ~lo_head, q2, 0.0).astype(BF16)
            s = lax.dot_general(qa, k2, NT_DIMS, preferred_element_type=F32) * (MEM_DIM ** -0.5)
            e = jnp.exp(s - jnp.max(s, axis=-1, keepdims=True))
            p = e / jnp.sum(e, axis=-1, keepdims=True)
            outs.append(_dot(p.astype(BF16), v2))
        o_ref[:, cols] = jnp.where(lo_head, outs[0], outs[1])


def _mem_attn(proj, q_col_block, kv, batch, seq):
    n = proj.shape[0]
    tq = min(MEM_ROWS, seq)
    nt = seq // tq
    return pl.pallas_call(
        _mem_attn_kernel,
        grid=(batch, nt),
        in_specs=[pl.BlockSpec((tq, MEM_W), lambda b, t: (b * nt + t, q_col_block)),
                  pl.BlockSpec((N_MEM, 2 * MEM_W), lambda b, t: (b, 0))],
        out_specs=pl.BlockSpec((tq, MEM_W), lambda b, t: (b * nt + t, 0)),
        out_shape=jax.ShapeDtypeStruct((n, MEM_W), F32),
        compiler_params=_params(("parallel", "parallel")),
        name="mem_attn",
    )(proj, kv)


SB_TILE = 256


def _sb_kernel(q_ref, k_ref, v_ref, o_ref, acc0_ref, acc1_ref, run0_ref, run1_ref):
    acc_refs = (acc0_ref, acc1_ref)
    run_refs = (run0_ref, run1_ref)
    i = pl.program_id(2)
    t = q_ref.shape[0]
    row = lax.broadcasted_iota(I32, (t, t), 0)
    col = lax.broadcasted_iota(I32, (t, t), 1)
    strict = col < row
    later = (row > col).astype(BF16)
    lo_head = lax.broadcasted_iota(I32, (t, LANES), 1) < SB_DIM
    q2 = q_ref[...] * (SB_DIM ** -0.5)
    qa = (jnp.where(lo_head, q2, 0.0).astype(BF16), jnp.where(lo_head, 0.0, q2).astype(BF16))
    for ref in acc_refs + run_refs:
        ref[...] = jnp.zeros_like(ref)

    def tile(j, diag):
        start = pl.multiple_of(j * t, t)
        kj = k_ref[pl.ds(start, t), :]
        vj = v_ref[pl.ds(start, t), :]
        heads = range(2)
        z = [lax.dot_general(qa[a], kj, NT_DIMS, preferred_element_type=F32) for a in heads]
        ls = [jnp.minimum(z[a], 0.0) - jnp.log(1.0 + jnp.exp(-jnp.abs(z[a]))) for a in heads]
        ln = [ls[a] - z[a] for a in heads]
        if diag:
            ln = [jnp.where(strict, ln[a], 0.0) for a in heads]
        exc = [sum(_dot(part, later) for part in _split_bf16(ln[a], 2)) for a in heads]
        run = [run_refs[a][...] for a in heads]
        a_w = [jnp.exp(ls[a] + exc[a] + jnp.concatenate([run[a]] * (t // LANES), axis=1))
               for a in heads]
        if diag:
            a_w = [jnp.where(strict, a_w[a], 0.0) for a in heads]
        out = [_dot(a_w[a].astype(BF16), vj) for a in heads]
        for a in heads:
            acc_refs[a][...] += out[a]
            total = exc[a][:, 0:1] + ln[a][:, 0:1]
            run_refs[a][...] = run[a] + jnp.broadcast_to(total, (t, LANES))

    tile(i, True)

    def body(s, carry):
        tile(i - 1 - s, False)
        return carry

    lax.fori_loop(0, i, body, 0)
    o_ref[...] = jnp.where(lo_head, acc0_ref[...], acc1_ref[...])


def _stick_breaking(qproj, kv, batch, seq):
    n = qproj.shape[0]
    t = min(SB_TILE, seq)
    nq = seq // t
    pairs = SB_W // LANES
    return pl.pallas_call(
        _sb_kernel,
        grid=(batch, pairs, nq),
        in_specs=[pl.BlockSpec((t, LANES), lambda b, j, i: (b * nq + i, j)),
                  pl.BlockSpec((seq, LANES), lambda b, j, i: (b, j)),
                  pl.BlockSpec((seq, LANES), lambda b, j, i: (b, pairs + j))],
        out_specs=pl.BlockSpec((t, LANES), lambda b, j, i: (b * nq + i, j)),
        out_shape=jax.ShapeDtypeStruct((n, SB_W), F32),
        scratch_shapes=[pltpu.VMEM((t, LANES), F32)] * 4,
        compiler_params=_params(("parallel", "parallel", "arbitrary")),
        name="stick_breaking",
    )(qproj, kv, kv)


LN_ROWS = 256


def _proj_ln_kernel(a1_ref, a2_ref, h_ref, w1_ref, w2_ref, g_ref, b_ref, o_ref, ob_ref):
    mix = _dot(a1_ref[...].astype(BF16), w1_ref[...]) + _dot(a2_ref[...].astype(BF16), w2_ref[...])
    y = _layer_norm(DEEPNORM_ALPHA * h_ref[...] + mix, g_ref[...], b_ref[...])
    o_ref[...] = y
    ob_ref[...] = y.astype(BF16)


def _ln_out(h, tm):
    spec = pl.BlockSpec((tm, h.shape[1]), lambda i: (i, 0))
    return [spec, spec], [jax.ShapeDtypeStruct(h.shape, F32), jax.ShapeDtypeStruct(h.shape, BF16)]


def _proj_ln(a1, a2, h, w1, w2, g, b):
    n = h.shape[0]
    tm = min(LN_ROWS, n)
    full = lambda arr: pl.BlockSpec(arr.shape, lambda i: (0, 0))
    rows = lambda arr: pl.BlockSpec((tm, arr.shape[1]), lambda i: (i, 0))
    out_specs, out_shape = _ln_out(h, tm)
    return pl.pallas_call(
        _proj_ln_kernel,
        grid=(n // tm,),
        in_specs=[rows(a1), rows(a2), rows(h), full(w1), full(w2), full(g), full(b)],
        out_specs=out_specs,
        out_shape=out_shape,
        compiler_params=_params(("parallel",)),
        name="proj_ln",
    )(a1, a2, h, w1, w2, g, b)


def _res_ln_kernel(f_ref, h_ref, g_ref, b_ref, o_ref, ob_ref):
    y = _layer_norm(DEEPNORM_ALPHA * h_ref[...] + f_ref[...], g_ref[...], b_ref[...])
    o_ref[...] = y
    ob_ref[...] = y.astype(BF16)


def _res_ln(f, h, g, b):
    n = h.shape[0]
    tm = min(512, n)
    full = lambda arr: pl.BlockSpec(arr.shape, lambda i: (0, 0))
    rows = lambda arr: pl.BlockSpec((tm, arr.shape[1]), lambda i: (i, 0))
    out_specs, out_shape = _ln_out(h, tm)
    return pl.pallas_call(
        _res_ln_kernel,
        grid=(n // tm,),
        in_specs=[rows(f), rows(h), full(g), full(b)],
        out_specs=out_specs,
        out_shape=out_shape,
        compiler_params=_params(("parallel",)),
        name="res_ln",
    )(f, h, g, b)


TOPK_TOKENS = 256


def _top16_rows(problems):
    vals = [v for v, _ in problems]
    iotas = [lax.broadcasted_iota(I32, v.shape, 0).astype(F32) for v in vals]
    outs = [([], []) for _ in problems]
    for _ in range(PEER_TOPK):
        for p, (_, payload) in enumerate(problems):
            m = jnp.max(vals[p], axis=0, keepdims=True)
            am = jnp.min(jnp.where(vals[p] == m, iotas[p], float(vals[p].shape[0])),
                         axis=0, keepdims=True)
            hit = iotas[p] == am
            outs[p][0].append(m)
            if payload is None:
                outs[p][1].append(am)
            else:
                outs[p][1].append(jnp.max(jnp.where(hit, payload, -1.0), axis=0, keepdims=True))
            vals[p] = jnp.where(hit, -jnp.inf, vals[p])
    return outs


def _pair_candidates(top0, top1):
    (s0, i0), (s1, i1) = top0, top1
    s0_all = jnp.concatenate(s0, axis=0)
    i0_all = jnp.concatenate(i0, axis=0)
    s1_all = jnp.concatenate(s1, axis=0)
    i1_all = jnp.concatenate(i1, axis=0)
    half = PEER_TOPK // 2
    sub = lax.broadcasted_iota(I32, (half, s0_all.shape[1]), 0)
    cand_s = [s0[0] + s1_all]
    cand_i = [i0[0] * float(PEER_KEYS) + i1_all]
    for i in range(1, half):
        cand_s.append(jnp.where(sub < PEER_TOPK // (i + 1), s0[i] + s1_all[:half], -jnp.inf))
        cand_i.append(i0[i] * float(PEER_KEYS) + i1_all[:half])
    cand_s.append(s0_all[half:] + s1[0])
    cand_i.append(i0_all[half:] * float(PEER_KEYS) + i1[0])
    return jnp.concatenate(cand_s, axis=0), jnp.concatenate(cand_i, axis=0)


def _peer_topk_kernel(q_ref, sk_ref, idx_ref, g_ref):
    groups = q_ref.shape[0] // LANES
    scores = []
    for grp in range(groups):
        for p in range(2):
            qp = q_ref[grp * LANES:(grp + 1) * LANES, p * PEER_QHALF:(p + 1) * PEER_QHALF]
            s_t = lax.dot_general(sk_ref[0, p], qp.astype(BF16), NT_DIMS, preferred_element_type=F32)
            scores.append((s_t, None))
    tops = _top16_rows(scores)
    best = _top16_rows([_pair_candidates(tops[2 * grp], tops[2 * grp + 1]) for grp in range(groups)])
    for grp, (best_s, best_i) in enumerate(best):
        cols = slice(grp * LANES, (grp + 1) * LANES)
        e = [jnp.exp(s - best_s[0]) for s in best_s]
        g_ref[:, cols] = jnp.concatenate(e, axis=0) / sum(e)
        idx_ref[:, cols] = (jnp.concatenate(best_i, axis=0) * float(ROW_WORDS)).astype(I32)


def _peer_topk(qp, subkeys):
    n = qp.shape[0]
    tt = min(TOPK_TOKENS, n)
    spec_out = pl.BlockSpec((PEER_TOPK, tt), lambda i, h: (h, i))
    return pl.pallas_call(
        _peer_topk_kernel,
        grid=(n // tt, PEER_HEADS),
        in_specs=[pl.BlockSpec((tt, 2 * PEER_QHALF), lambda i, h: (i, h)),
                  pl.BlockSpec((1, 2, PEER_KEYS, PEER_QHALF), lambda i, h: (h, 0, 0, 0))],
        out_specs=[spec_out, spec_out],
        out_shape=[jax.ShapeDtypeStruct((PEER_PICKS, n), I32),
                   jax.ShapeDtypeStruct((PEER_PICKS, n), F32)],
        compiler_params=_params(("parallel", "parallel")),
        name="peer_topk",
    )(qp, subkeys)


def _pack_kernel(x_ref, o_ref):
    o_ref[...] = pltpu.bitcast(x_ref[...].astype(BF16), I32)


def _pack_rows(x2d):
    r = x2d.shape[0]
    tr = min(4096, r)
    return pl.pallas_call(
        _pack_kernel,
        grid=(r // tr,),
        in_specs=[pl.BlockSpec((tr, LANES), lambda i: (i, 0))],
        out_specs=pl.BlockSpec((tr // 2, LANES), lambda i: (i, 0)),
        out_shape=jax.ShapeDtypeStruct((r // 2, LANES), I32),
        compiler_params=_params(("parallel",)),
        name="pack_bf16",
    )(x2d)


def _pack_table(tab):
    e = tab.shape[0]
    return _pack_rows(tab.reshape(e * ROW_CHUNKS, LANES))


PEER_TOKENS = 64
PEER_UNROLL = 8
PAIRS = PEER_PICKS // 2


def _gather_rows(idx_ref, tab_ref, n):
    pieces = []
    for k in range(PAIRS):
        e0 = idx_ref[0, n, 2 * k]
        e1 = idx_ref[0, n, 2 * k + 1]
        words = jnp.concatenate(
            [tab_ref[pl.ds(pl.multiple_of(e, ROW_WORDS), ROW_WORDS), :] for e in (e0, e1)],
            axis=0)
        pieces.append(pltpu.bitcast(words, BF16))
    return pieces


def _token_loop(n_tokens, token):
    def body(m, carry):
        for u in range(PEER_UNROLL):
            token(m * PEER_UNROLL + u)
        return carry

    lax.fori_loop(0, n_tokens // PEER_UNROLL, body, 0)


def _peer_u_kernel(idx_ref, x_ref, tab_ref, o_ref):
    ones = jnp.ones((LANES, LANES), BF16)
    lane = lax.broadcasted_iota(I32, (ROW_CHUNKS, LANES), 1)

    def token(n):
        xw = pltpu.bitcast(jnp.concatenate([x_ref[n], x_ref[n]], axis=0), BF16)
        prods = [piece * xw for piece in _gather_rows(idx_ref, tab_ref, n)]
        z = _dot(jnp.concatenate(prods, axis=0), ones)
        n_acc = 4
        acc = [jnp.zeros((ROW_CHUNKS, LANES), F32) for _ in range(n_acc)]
        for e in range(PEER_PICKS):
            acc[e % n_acc] = acc[e % n_acc] + jnp.where(
                lane == e, z[e * ROW_CHUNKS:(e + 1) * ROW_CHUNKS, :], 0.0)
        o_ref[n] = jnp.sum(sum(acc), axis=0, keepdims=True)

    _token_loop(o_ref.shape[0], token)


def _peer_v_kernel(idx_ref, wexp_ref, tab_ref, o_ref):
    lane = lax.broadcasted_iota(I32, (ROW_CHUNKS, LANES), 1)
    sub = lax.broadcasted_iota(I32, (ROW_CHUNKS, LANES), 0)
    own_chunk = (lane & (ROW_CHUNKS - 1)) == sub

    def token(n):
        rows = jnp.concatenate(_gather_rows(idx_ref, tab_ref, n), axis=0)
        lhs = jnp.concatenate(
            [jnp.where(own_chunk, jnp.broadcast_to(wexp_ref[j, n], (ROW_CHUNKS, LANES)), 0.0)
             for j in range(ROW_CHUNKS)], axis=1).astype(BF16)
        o_ref[n] = _dot(lhs, rows)

    _token_loop(o_ref.shape[0], token)


def _idx_spec(tb):
    return pl.BlockSpec((1, tb, PEER_PICKS), lambda i: (i, 0, 0), memory_space=pltpu.SMEM)


def _table_spec(tab):
    return pl.BlockSpec(tab.shape, lambda i: (0, 0), pipeline_mode=pl.Buffered(1))


def _peer_u(idx, xw, tab):
    n = idx.shape[0]
    tb = min(PEER_TOKENS, n)
    out = pl.pallas_call(
        _peer_u_kernel,
        grid=(n // tb,),
        in_specs=[_idx_spec(tb), pl.BlockSpec((tb, ROW_WORDS, LANES), lambda i: (i, 0, 0)),
                  _table_spec(tab)],
        out_specs=pl.BlockSpec((tb, 1, LANES), lambda i: (i, 0, 0)),
        out_shape=jax.ShapeDtypeStruct((n, 1, LANES), F32),
        compiler_params=_params(("arbitrary",), VMEM_LIMIT_TABLE),
        name="peer_u",
    )(idx.reshape(n // tb, tb, PEER_PICKS), xw, tab)
    return out.reshape(n, LANES)


def _peer_v(idx, wexp, tab):
    n = idx.shape[0]
    tb = min(PEER_TOKENS, n)
    out = pl.pallas_call(
        _peer_v_kernel,
        grid=(n // tb,),
        in_specs=[_idx_spec(tb),
                  pl.BlockSpec((ROW_CHUNKS, tb, 1, LANES), lambda i: (0, i, 0, 0)),
                  _table_spec(tab)],
        out_specs=pl.BlockSpec((tb, ROW_CHUNKS, LANES), lambda i: (i, 0, 0)),
        out_shape=jax.ShapeDtypeStruct((n, ROW_CHUNKS, LANES), F32),
        compiler_params=_params(("arbitrary",), VMEM_LIMIT_TABLE),
        name="peer_v",
    )(idx.reshape(n // tb, tb, PEER_PICKS), wexp, tab)
    return out.reshape(n, D_MODEL)


def _gate_act_kernel(a_ref, g_ref, e_ref, o_ref):
    a = a_ref[...]
    w = g_ref[...] * (0.5 * a * (1.0 + lax.erf(a * (2.0 ** -0.5))))
    wx = _dot(w.astype(BF16), e_ref[...])
    for j in range(ROW_CHUNKS):
        o_ref[j] = wx[:, j * LANES:(j + 1) * LANES]


def _gate_act(act, g):
    n = act.shape[0]
    tm = min(1024, n)
    spec = pl.BlockSpec((tm, LANES), lambda i: (i, 0))
    expand = (jnp.arange(D_MODEL)[None, :] // ROW_CHUNKS == jnp.arange(LANES)[:, None]).astype(BF16)
    return pl.pallas_call(
        _gate_act_kernel,
        grid=(n // tm,),
        in_specs=[spec, spec, pl.BlockSpec((LANES, D_MODEL), lambda i: (0, 0))],
        out_specs=pl.BlockSpec((ROW_CHUNKS, tm, LANES), lambda i: (0, i, 0)),
        out_shape=jax.ShapeDtypeStruct((ROW_CHUNKS, n, LANES), F32),
        compiler_params=_params(("parallel",)),
        name="peer_gate_act",
    )(act, g, expand)


def _peer(hb, w_q, subkeys, u_tab, v_tab):
    n = hb.shape[0]
    qp = _matmul(hb, w_q.astype(BF16))
    idx_t, g_t = _peer_topk(qp, subkeys.astype(BF16))
    idx = idx_t.T
    g = g_t.T
    xw = lax.bitcast_convert_type(
        hb.reshape(n, ROW_WORDS, 2, LANES).transpose(0, 1, 3, 2), I32)
    act = _peer_u(idx, xw, _pack_table(u_tab))
    wexp = _gate_act(act, g).reshape(ROW_CHUNKS, n, 1, LANES)
    return _peer_v(idx, wexp, _pack_table(v_tab))


def kernel(x, mem, a_w_in, a_w_gate2, a_b_gate, a_head_g, a_w_mem_kv, a_w_out, b_w_in, b_w_mem_kv,
           b_w_out, sb_w_kv, peer_w_q, peer_subkeys, peer_u, peer_v, ln_g, ln_b):
    batch, seq, d = x.shape
    n = batch * seq
    h = x.reshape(n, d)
    memf = mem.reshape(batch * N_MEM, d)
    pad_w = 2 * GLA_DV
    s = [GLA_K, 2 * GLA_K, 2 * GLA_K + GLA_V, 2 * GLA_K + 2 * GLA_V,
         2 * GLA_K + 2 * GLA_V + GLA_GATE_RANK]

    w_in = a_w_in[0]
    w_in_r = jnp.concatenate(
        [w_in[:, :s[3]], w_in[:, s[4]:], w_in[:, s[3]:s[4]],
         jnp.zeros((d, pad_w - GLA_GATE_RANK), F32)], axis=1).astype(BF16)
    proj = _matmul(h, w_in_r)
    wg = jnp.concatenate([a_w_gate2[0], jnp.zeros((pad_w - GLA_GATE_RANK, GLA_K), F32)],
                         axis=0).astype(BF16)
    o = _gla(proj, wg, a_b_gate[0].reshape(1, GLA_K), a_head_g[0].reshape(1, GLA_V), batch, seq)
    kv_mem = _matmul(memf, a_w_mem_kv[0].astype(BF16))
    m = _mem_attn(proj, (2 * GLA_K + 2 * GLA_V) // MEM_W, kv_mem, batch, seq)
    w_out = a_w_out[0].astype(BF16)
    h, hb = _proj_ln(o, m, h, w_out[:GLA_V], w_out[GLA_V:], ln_g[0, 0].reshape(1, d),
                     ln_b[0, 0].reshape(1, d))
    ffn = _peer(hb, peer_w_q[0], peer_subkeys[0], peer_u[0], peer_v[0])
    h, hb = _res_ln(ffn, h, ln_g[0, 1].reshape(1, d), ln_b[0, 1].reshape(1, d))

    kv_sb = _matmul(hb, sb_w_kv.astype(BF16), out_dtype=BF16)
    proj = _matmul(hb, b_w_in[0].astype(BF16))
    o = _stick_breaking(proj, kv_sb, batch, seq)
    kv_mem = _matmul(memf, b_w_mem_kv[0].astype(BF16))
    m = _mem_attn(proj, SB_W // MEM_W, kv_mem, batch, seq)
    w_out = b_w_out[0].astype(BF16)
    h, hb = _proj_ln(o, m, h, w_out[:SB_W], w_out[SB_W:], ln_g[1, 0].reshape(1, d),
                     ln_b[1, 0].reshape(1, d))
    ffn = _peer(hb, peer_w_q[1], peer_subkeys[1], peer_u[1], peer_v[1])
    h, _ = _res_ln(ffn, h, ln_g[1, 1].reshape(1, d), ln_b[1, 1].reshape(1, d))
    return h.reshape(batch, seq, d)
```

```python
import functools

import jax
import jax.numpy as jnp
from jax import lax
from jax.experimental import pallas as pl
from jax.experimental.pallas import tpu as pltpu

F32 = jnp.float32
BF16 = jnp.bfloat16
I32 = jnp.int32

LANES = 128
SUBLANES = 8
VMEM_LIMIT_DEFAULT = 48 * 1024 * 1024
VMEM_LIMIT_TABLE = 56 * 1024 * 1024

D_MODEL = 1024
N_MEM = 256
GLA_HEADS = 6
GLA_DK = 64
GLA_DV = 128
GLA_K = GLA_HEADS * GLA_DK
GLA_V = GLA_HEADS * GLA_DV
GLA_GATE_RANK = 16
GLA_TAU = 16.0
GLA_CHUNK = 64
SB_HEADS = 12
SB_DIM = 64
SB_W = SB_HEADS * SB_DIM
SB_BLOCK = 128
MEM_HEADS = 4
MEM_DIM = 64
MEM_W = MEM_HEADS * MEM_DIM
PEER_HEADS = 8
PEER_KEYS = 128
PEER_EXPERTS = PEER_KEYS * PEER_KEYS
PEER_TOPK = 16
PEER_QHALF = 128
PEER_PICKS = PEER_HEADS * PEER_TOPK
DEPTH = 2
DEEPNORM_ALPHA = (2.0 * DEPTH) ** 0.25
EPS = 1e-5

ROW_CHUNKS = D_MODEL // LANES
ROW_WORDS = ROW_CHUNKS // 2

NT_DIMS = (((1,), (1,)), ((), ()))
TN_DIMS = (((0,), (0,)), ((), ()))


def _params(semantics, vmem=VMEM_LIMIT_DEFAULT):
    return pltpu.CompilerParams(dimension_semantics=semantics, vmem_limit_bytes=vmem)


def _log_sigmoid(z):
    return jnp.minimum(z, 0.0) - jnp.log1p(jnp.exp(-jnp.abs(z)))


def _dot(a, b):
    return jnp.dot(a, b, preferred_element_type=F32)


def _split_bf16(x, parts):
    out = []
    r = x
    for _ in range(parts):
        p = r.astype(BF16)
        out.append(p)
        r = r - p.astype(F32)
    return out


def _layer_norm(y, g, b):
    mu = jnp.mean(y, axis=-1, keepdims=True)
    yc = y - mu
    var = jnp.mean(yc * yc, axis=-1, keepdims=True)
    return yc * lax.rsqrt(var + EPS) * g + b


def _matmul_kernel(x_ref, w_ref, o_ref):
    o_ref[...] = _dot(x_ref[...].astype(BF16), w_ref[...]).astype(o_ref.dtype)


MATMUL_ROWS = 512


def _matmul(x, w, out_dtype=F32):
    m, k = x.shape
    n = w.shape[1]
    tm = min(MATMUL_ROWS, m)
    return pl.pallas_call(
        _matmul_kernel,
        grid=(m // tm,),
        in_specs=[pl.BlockSpec((tm, k), lambda i: (i, 0)),
                  pl.BlockSpec((k, n), lambda i: (0, 0))],
        out_specs=pl.BlockSpec((tm, n), lambda i: (i, 0)),
        out_shape=jax.ShapeDtypeStruct((m, n), out_dtype),
        compiler_params=_params(("parallel",)),
        name="matmul",
    )(x, w)


GLA_ROWS = 256


def _gla_kernel(q_ref, k_ref, v_ref, r_ref, glr_ref, wg_ref, bg_ref, hg_ref, o_ref, st_ref):
    @pl.when(pl.program_id(2) == 0)
    def _():
        st_ref[...] = jnp.zeros_like(st_ref)

    c_rows = GLA_CHUNK
    lane = lax.broadcasted_iota(I32, (c_rows, LANES), 1)
    head_masks = (lane < GLA_DK, lane >= GLA_DK)
    row = lax.broadcasted_iota(I32, (c_rows, c_rows), 0)
    col = lax.broadcasted_iota(I32, (c_rows, c_rows), 1)
    causal = row >= col
    tril = causal.astype(BF16)
    wg = wg_ref[...]
    bg = bg_ref[...]
    hg = hg_ref[...]
    chunks = range(q_ref.shape[0] // c_rows)
    heads = range(2)
    rows = [slice(c * c_rows, (c + 1) * c_rows) for c in chunks]
    cols = [slice(a * GLA_DV, (a + 1) * GLA_DV) for a in heads]
    z = [_dot(glr_ref[rows[c], :].astype(BF16), wg) + bg for c in chunks]
    lg_parts = [_split_bf16(_log_sigmoid(z[c]) * (1.0 / GLA_TAU), 3) for c in chunks]
    bc = [sum(_dot(tril, part) for part in lg_parts[c]) for c in chunks]
    bl = [bc[c][c_rows - 1:c_rows, :] for c in chunks]
    qd = [q_ref[rows[c], :] * (GLA_DK ** -0.5) * jnp.exp(bc[c]) for c in chunks]
    ki = [(k_ref[rows[c], :] * jnp.exp(-bc[c])).astype(BF16) for c in chunks]
    ke = [k_ref[rows[c], :] * jnp.exp(bl[c] - bc[c]) for c in chunks]
    dec = [jnp.exp(bl[c]) for c in chunks]
    qa = [[jnp.where(head_masks[a], qd[c], 0.0).astype(BF16) for a in heads] for c in chunks]
    kea = [[jnp.where(head_masks[a], ke[c], 0.0).astype(BF16) for a in heads] for c in chunks]
    va = [[v_ref[rows[c], cols[a]].astype(BF16) for a in heads] for c in chunks]
    attn = [[jnp.where(causal, lax.dot_general(qa[c][a], ki[c], NT_DIMS, preferred_element_type=F32),
                       0.0).astype(BF16) for a in heads] for c in chunks]
    inc = [[lax.dot_general(va[c][a], kea[c][a], TN_DIMS, preferred_element_type=F32)
            for a in heads] for c in chunks]
    o_intra = [[_dot(attn[c][a], va[c][a]) for a in heads] for c in chunks]
    states = [st_ref[0], st_ref[1]]
    for c in chunks:
        for a in heads:
            o = o_intra[c][a] + lax.dot_general(qa[c][a], states[a].astype(BF16), NT_DIMS,
                                                preferred_element_type=F32)
            states[a] = states[a] * dec[c] + inc[c][a]
            ms = jnp.mean(o * o, axis=-1, keepdims=True)
            r = r_ref[rows[c], cols[a]]
            silu = r / (1.0 + jnp.exp(-r))
            o_ref[rows[c], cols[a]] = o * lax.rsqrt(ms + EPS) * hg[:, cols[a]] * silu
    st_ref[0] = states[0]
    st_ref[1] = states[1]


def _gla(proj, wg, bg, hg, batch, seq):
    n = proj.shape[0]
    tr = min(GLA_ROWS, seq)
    nt = seq // tr
    rowmap = lambda b, j, t: b * nt + t
    pair_w = 2 * GLA_DV
    return pl.pallas_call(
        _gla_kernel,
        grid=(batch, GLA_HEADS // 2, nt),
        in_specs=[
            pl.BlockSpec((tr, LANES), lambda b, j, t: (rowmap(b, j, t), j)),
            pl.BlockSpec((tr, LANES), lambda b, j, t: (rowmap(b, j, t), GLA_K // LANES + j)),
            pl.BlockSpec((tr, pair_w), lambda b, j, t: (rowmap(b, j, t), 2 * GLA_K // pair_w + j)),
            pl.BlockSpec((tr, pair_w),
                         lambda b, j, t: (rowmap(b, j, t), (2 * GLA_K + GLA_V) // pair_w + j)),
            pl.BlockSpec((tr, pair_w),
                         lambda b, j, t: (rowmap(b, j, t), (2 * GLA_K + 2 * GLA_V + MEM_W) // pair_w)),
            pl.BlockSpec((pair_w, LANES), lambda b, j, t: (0, j)),
            pl.BlockSpec((1, LANES), lambda b, j, t: (0, j)),
            pl.BlockSpec((1, pair_w), lambda b, j, t: (0, j)),
        ],
        out_specs=pl.BlockSpec((tr, pair_w), lambda b, j, t: (rowmap(b, j, t), j)),
        out_shape=jax.ShapeDtypeStruct((n, GLA_V), F32),
        scratch_shapes=[pltpu.VMEM((2, GLA_DV, LANES), F32)],
        compiler_params=_params(("parallel", "parallel", "arbitrary")),
        name="gla",
    )(proj, proj, proj, proj, proj, wg, bg, hg)


MEM_ROWS = 512


def _mem_attn_kernel(q_ref, kv_ref, o_ref):
    tq = q_ref.shape[0]
    lane = lax.broadcasted_iota(I32, (tq, LANES), 1)
    lo_head = lane < MEM_DIM
    for j in range(MEM_W // LANES):
        cols = slice(j * LANES, (j + 1) * LANES)
        q2 = q_ref[:, cols]
        k2 = kv_ref[:, cols].astype(BF16)
        v2 = kv_ref[:, MEM_W + j * LANES:MEM_W + (j + 1) * LANES].astype(BF16)
        outs = []
        for a in range(2):
            qa = jnp.where(lo_head if a == 0 else ~lo_head, q2, 0.0).astype(BF16)
            s = lax.dot_general(qa, k2, NT_DIMS, preferred_element_type=F32) * (MEM_DIM ** -0.5)
            e = jnp.exp(s - jnp.max(s, axis=-1, keepdims=True))
            p = e / jnp.sum(e, axis=-1, keepdims=True)
            outs.append(_dot(p.astype(BF16), v2))
        o_ref[:, cols] = jnp.where(lo_head, outs[0], outs[1])


def _mem_attn(proj, q_col_block, kv, batch, seq):
    n = proj.shape[0]
    tq = min(MEM_ROWS, seq)
    nt = seq // tq
    return pl.pallas_call(
        _mem_attn_kernel,
        grid=(batch, nt),
        in_specs=[pl.BlockSpec((tq, MEM_W), lambda b, t: (b * nt + t, q_col_block)),
                  pl.BlockSpec((N_MEM, 2 * MEM_W), lambda b, t: (b, 0))],
        out_specs=pl.BlockSpec((tq, MEM_W), lambda b, t: (b * nt + t, 0)),
        out_shape=jax.ShapeDtypeStruct((n, MEM_W), F32),
        compiler_params=_params(("parallel", "parallel")),
        name="mem_attn",
    )(proj, kv)


SB_TILE = 256


SB_MASKED = -1e30


def _sb_kernel(q_ref, k_ref, v_ref, o_ref, *scratch):
    acc_refs, run_refs = scratch[0:2], scratch[2:4]
    stage = (scratch[4:10], scratch[10:16])
    i = pl.program_id(2)
    t = q_ref.shape[0]
    heads = range(2)
    row = lax.broadcasted_iota(I32, (t, t), 0)
    col = lax.broadcasted_iota(I32, (t, t), 1)
    later = (row > col).astype(BF16)
    lo_head = lax.broadcasted_iota(I32, (t, LANES), 1) < SB_DIM
    q2 = q_ref[...] * (SB_DIM ** -0.5)
    qa = (jnp.where(lo_head, q2, 0.0).astype(BF16), jnp.where(lo_head, 0.0, q2).astype(BF16))
    for ref in acc_refs + run_refs:
        ref[...] = jnp.zeros_like(ref)

    def scores(j, p, diag):
        kj = k_ref[pl.ds(pl.multiple_of(j * t, t), t), :]
        z = [lax.dot_general(qa[a], kj, NT_DIMS, preferred_element_type=F32) for a in heads]
        for a in heads:
            ls = jnp.minimum(z[a], 0.0) - jnp.log(1.0 + jnp.exp(-jnp.abs(z[a])))
            ln = ls - z[a]
            if diag:
                visible = col < row
                ln = jnp.where(visible, ln, 0.0)
                ls = jnp.where(visible, ls, SB_MASKED)
            hi, lo = _split_bf16(ln, 2)
            stage[p][a][...] = ls
            stage[p][2 + a][...] = hi
            stage[p][4 + a][...] = lo

    def apply(j, p):
        vj = v_ref[pl.ds(pl.multiple_of(j * t, t), t), :]
        hi = [stage[p][2 + a][...] for a in heads]
        lo = [stage[p][4 + a][...] for a in heads]
        exc = [_dot(hi[a], later) + _dot(lo[a], later) for a in heads]
        run = [run_refs[a][...] for a in heads]
        a_w = [jnp.exp(stage[p][a][...] + exc[a] + jnp.concatenate([run[a]] * (t // LANES), axis=1))
               for a in heads]
        out = [_dot(a_w[a].astype(BF16), vj) for a in heads]
        for a in heads:
            acc_refs[a][...] += out[a]
            total = exc[a][:, 0:1] + hi[a][:, 0:1].astype(F32) + lo[a][:, 0:1].astype(F32)
            run_refs[a][...] = run[a] + jnp.broadcast_to(total, (t, LANES))

    scores(i, 0, True)

    def two_steps(m, carry):
        j = i - 2 * m
        scores(j - 1, 1, False)
        apply(j, 0)
        scores(j - 2, 0, False)
        apply(j - 1, 1)
        return carry

    lax.fori_loop(0, i // 2, two_steps, 0)

    @pl.when(i % 2 == 1)
    def _():
        scores(0, 1, False)
        apply(1, 0)
        apply(0, 1)

    @pl.when(i % 2 == 0)
    def _():
        apply(0, 0)

    o_ref[...] = jnp.where(lo_head, acc_refs[0][...], acc_refs[1][...])


def _stick_breaking(qproj, kv, batch, seq):
    n = qproj.shape[0]
    t = min(SB_TILE, seq)
    nq = seq // t
    pairs = SB_W // LANES
    return pl.pallas_call(
        _sb_kernel,
        grid=(batch, pairs, nq),
        in_specs=[pl.BlockSpec((t, LANES), lambda b, j, i: (b * nq + i, j)),
                  pl.BlockSpec((seq, LANES), lambda b, j, i: (b, j)),
                  pl.BlockSpec((seq, LANES), lambda b, j, i: (b, pairs + j))],
        out_specs=pl.BlockSpec((t, LANES), lambda b, j, i: (b * nq + i, j)),
        out_shape=jax.ShapeDtypeStruct((n, SB_W), F32),
        scratch_shapes=([pltpu.VMEM((t, LANES), F32)] * 4
                        + ([pltpu.VMEM((t, t), F32)] * 2 + [pltpu.VMEM((t, t), BF16)] * 4) * 2),
        compiler_params=_params(("parallel", "parallel", "arbitrary")),
        name="stick_breaking",
    )(qproj, kv, kv)


LN_ROWS = 256


def _proj_ln_kernel(a1_ref, a2_ref, h_ref, w1_ref, w2_ref, g_ref, b_ref, o_ref, ob_ref):
    mix = _dot(a1_ref[...].astype(BF16), w1_ref[...]) + _dot(a2_ref[...].astype(BF16), w2_ref[...])
    y = _layer_norm(DEEPNORM_ALPHA * h_ref[...] + mix, g_ref[...], b_ref[...])
    o_ref[...] = y
    ob_ref[...] = y.astype(BF16)


def _ln_out(h, tm):
    spec = pl.BlockSpec((tm, h.shape[1]), lambda i: (i, 0))
    return [spec, spec], [jax.ShapeDtypeStruct(h.shape, F32), jax.ShapeDtypeStruct(h.shape, BF16)]


def _proj_ln(a1, a2, h, w1, w2, g, b):
    n = h.shape[0]
    tm = min(LN_ROWS, n)
    full = lambda arr: pl.BlockSpec(arr.shape, lambda i: (0, 0))
    rows = lambda arr: pl.BlockSpec((tm, arr.shape[1]), lambda i: (i, 0))
    out_specs, out_shape = _ln_out(h, tm)
    return pl.pallas_call(
        _proj_ln_kernel,
        grid=(n // tm,),
        in_specs=[rows(a1), rows(a2), rows(h), full(w1), full(w2), full(g), full(b)],
        out_specs=out_specs,
        out_shape=out_shape,
        compiler_params=_params(("parallel",)),
        name="proj_ln",
    )(a1, a2, h, w1, w2, g, b)


def _res_ln_kernel(f_ref, h_ref, g_ref, b_ref, o_ref, ob_ref):
    y = _layer_norm(DEEPNORM_ALPHA * h_ref[...] + f_ref[...], g_ref[...], b_ref[...])
    o_ref[...] = y
    ob_ref[...] = y.astype(BF16)


def _res_ln(f, h, g, b):
    n = h.shape[0]
    tm = min(512, n)
    full = lambda arr: pl.BlockSpec(arr.shape, lambda i: (0, 0))
    rows = lambda arr: pl.BlockSpec((tm, arr.shape[1]), lambda i: (i, 0))
    out_specs, out_shape = _ln_out(h, tm)
    return pl.pallas_call(
        _res_ln_kernel,
        grid=(n // tm,),
        in_specs=[rows(f), rows(h), full(g), full(b)],
        out_specs=out_specs,
        out_shape=out_shape,
        compiler_params=_params(("parallel",)),
        name="res_ln",
    )(f, h, g, b)


TOPK_TOKENS = 256


def _top16_rows(problems):
    vals = [v for v, _ in problems]
    iotas = [lax.broadcasted_iota(I32, v.shape, 0).astype(F32) for v in vals]
    outs = [([], []) for _ in problems]
    for _ in range(PEER_TOPK):
        for p, (_, payload) in enumerate(problems):
            m = jnp.max(vals[p], axis=0, keepdims=True)
            am = jnp.min(jnp.where(vals[p] == m, iotas[p], float(vals[p].shape[0])),
                         axis=0, keepdims=True)
            hit = iotas[p] == am
            outs[p][0].append(m)
            if payload is None:
                outs[p][1].append(am)
            else:
                outs[p][1].append(jnp.max(jnp.where(hit, payload, -1.0), axis=0, keepdims=True))
            vals[p] = jnp.where(hit, -jnp.inf, vals[p])
    return outs


def _pair_candidates(top0, top1):
    (s0, i0), (s1, i1) = top0, top1
    s0_all = jnp.concatenate(s0, axis=0)
    i0_all = jnp.concatenate(i0, axis=0)
    s1_all = jnp.concatenate(s1, axis=0)
    i1_all = jnp.concatenate(i1, axis=0)
    half = PEER_TOPK // 2
    sub = lax.broadcasted_iota(I32, (half, s0_all.shape[1]), 0)
    cand_s = [s0[0] + s1_all]
    cand_i = [i0[0] * float(PEER_KEYS) + i1_all]
    for i in range(1, half):
        cand_s.append(jnp.where(sub < PEER_TOPK // (i + 1), s0[i] + s1_all[:half], -jnp.inf))
        cand_i.append(i0[i] * float(PEER_KEYS) + i1_all[:half])
    cand_s.append(s0_all[half:] + s1[0])
    cand_i.append(i0_all[half:] * float(PEER_KEYS) + i1[0])
    return jnp.concatenate(cand_s, axis=0), jnp.concatenate(cand_i, axis=0)


def _peer_topk_kernel(q_ref, sk_ref, idx_ref, g_ref):
    groups = q_ref.shape[0] // LANES
    scores = []
    for grp in range(groups):
        for p in range(2):
            qp = q_ref[grp * LANES:(grp + 1) * LANES, p * PEER_QHALF:(p + 1) * PEER_QHALF]
            s_t = lax.dot_general(sk_ref[0, p], qp.astype(BF16), NT_DIMS, preferred_element_type=F32)
            scores.append((s_t, None))
    tops = _top16_rows(scores)
    best = _top16_rows([_pair_candidates(tops[2 * grp], tops[2 * grp + 1]) for grp in range(groups)])
    for grp, (best_s, best_i) in enumerate(best):
        cols = slice(grp * LANES, (grp + 1) * LANES)
        e = [jnp.exp(s - best_s[0]) for s in best_s]
        g_ref[:, cols] = jnp.concatenate(e, axis=0) / sum(e)
        idx_ref[:, cols] = (jnp.concatenate(best_i, axis=0) * float(ROW_WORDS)).astype(I32)


def _peer_topk(qp, subkeys):
    n = qp.shape[0]
    tt = min(TOPK_TOKENS, n)
    spec_out = pl.BlockSpec((PEER_TOPK, tt), lambda i, h: (h, i))
    return pl.pallas_call(
        _peer_topk_kernel,
        grid=(n // tt, PEER_HEADS),
        in_specs=[pl.BlockSpec((tt, 2 * PEER_QHALF), lambda i, h: (i, h)),
                  pl.BlockSpec((1, 2, PEER_KEYS, PEER_QHALF), lambda i, h: (h, 0, 0, 0))],
        out_specs=[spec_out, spec_out],
        out_shape=[jax.ShapeDtypeStruct((PEER_PICKS, n), I32),
                   jax.ShapeDtypeStruct((PEER_PICKS, n), F32)],
        compiler_params=_params(("parallel", "parallel")),
        name="peer_topk",
    )(qp, subkeys)


def _pack_kernel(x_ref, o_ref):
    o_ref[...] = pltpu.bitcast(x_ref[...].astype(BF16), I32)


def _pack_rows(x2d):
    r = x2d.shape[0]
    tr = min(4096, r)
    return pl.pallas_call(
        _pack_kernel,
        grid=(r // tr,),
        in_specs=[pl.BlockSpec((tr, LANES), lambda i: (i, 0))],
        out_specs=pl.BlockSpec((tr // 2, LANES), lambda i: (i, 0)),
        out_shape=jax.ShapeDtypeStruct((r // 2, LANES), I32),
        compiler_params=_params(("parallel",)),
        name="pack_bf16",
    )(x2d)


def _pack_table(tab):
    e = tab.shape[0]
    return _pack_rows(tab.reshape(e * ROW_CHUNKS, LANES))


PEER_TOKENS = 64
PEER_UNROLL = 8
PAIRS = PEER_PICKS // 2


def _gather_rows(idx_ref, tab_ref, n):
    pieces = []
    for k in range(PAIRS):
        e0 = idx_ref[0, n, 2 * k]
        e1 = idx_ref[0, n, 2 * k + 1]
        words = jnp.concatenate(
            [tab_ref[pl.ds(pl.multiple_of(e, ROW_WORDS), ROW_WORDS), :] for e in (e0, e1)],
            axis=0)
        pieces.append(pltpu.bitcast(words, BF16))
    return pieces


def _token_loop(n_tokens, token):
    def body(m, carry):
        for u in range(PEER_UNROLL):
            token(m * PEER_UNROLL + u)
        return carry

    lax.fori_loop(0, n_tokens // PEER_UNROLL, body, 0)


def _peer_u_kernel(idx_ref, x_ref, tab_ref, o_ref):
    ones = jnp.ones((LANES, LANES), BF16)
    lane = lax.broadcasted_iota(I32, (ROW_CHUNKS, LANES), 1)

    def token(n):
        xw = pltpu.bitcast(jnp.concatenate([x_ref[n], x_ref[n]], axis=0), BF16)
        prods = [piece * xw for piece in _gather_rows(idx_ref, tab_ref, n)]
        z = _dot(jnp.concatenate(prods, axis=0), ones)
        n_acc = 4
        acc = [jnp.zeros((ROW_CHUNKS, LANES), F32) for _ in range(n_acc)]
        for e in range(PEER_PICKS):
            acc[e % n_acc] = acc[e % n_acc] + jnp.where(
                lane == e, z[e * ROW_CHUNKS:(e + 1) * ROW_CHUNKS, :], 0.0)
        o_ref[n] = jnp.sum(sum(acc), axis=0, keepdims=True)

    _token_loop(o_ref.shape[0], token)


def _peer_v_kernel(idx_ref, wexp_ref, tab_ref, o_ref):
    lane = lax.broadcasted_iota(I32, (ROW_CHUNKS, LANES), 1)
    sub = lax.broadcasted_iota(I32, (ROW_CHUNKS, LANES), 0)
    own_chunk = (lane & (ROW_CHUNKS - 1)) == sub

    def token(n):
        rows = jnp.concatenate(_gather_rows(idx_ref, tab_ref, n), axis=0)
        lhs = jnp.concatenate(
            [jnp.where(own_chunk, jnp.broadcast_to(wexp_ref[j, n], (ROW_CHUNKS, LANES)), 0.0)
             for j in range(ROW_CHUNKS)], axis=1).astype(BF16)
        o_ref[n] = _dot(lhs, rows)

    _token_loop(o_ref.shape[0], token)


def _idx_spec(tb):
    return pl.BlockSpec((1, tb, PEER_PICKS), lambda i: (i, 0, 0), memory_space=pltpu.SMEM)


def _table_spec(tab):
    return pl.BlockSpec(tab.shape, lambda i: (0, 0), pipeline_mode=pl.Buffered(1))


def _peer_u(idx, xw, tab):
    n = idx.shape[0]
    tb = min(PEER_TOKENS, n)
    out = pl.pallas_call(
        _peer_u_kernel,
        grid=(n // tb,),
        in_specs=[_idx_spec(tb), pl.BlockSpec((tb, ROW_WORDS, LANES), lambda i: (i, 0, 0)),
                  _table_spec(tab)],
        out_specs=pl.BlockSpec((tb, 1, LANES), lambda i: (i, 0, 0)),
        out_shape=jax.ShapeDtypeStruct((n, 1, LANES), F32),
        compiler_params=_params(("arbitrary",), VMEM_LIMIT_TABLE),
        name="peer_u",
    )(idx.reshape(n // tb, tb, PEER_PICKS), xw, tab)
    return out.reshape(n, LANES)


def _peer_v(idx, wexp, tab):
    n = idx.shape[0]
    tb = min(PEER_TOKENS, n)
    out = pl.pallas_call(
        _peer_v_kernel,
        grid=(n // tb,),
        in_specs=[_idx_spec(tb),
                  pl.BlockSpec((ROW_CHUNKS, tb, 1, LANES), lambda i: (0, i, 0, 0)),
                  _table_spec(tab)],
        out_specs=pl.BlockSpec((tb, ROW_CHUNKS, LANES), lambda i: (i, 0, 0)),
        out_shape=jax.ShapeDtypeStruct((n, ROW_CHUNKS, LANES), F32),
        compiler_params=_params(("arbitrary",), VMEM_LIMIT_TABLE),
        name="peer_v",
    )(idx.reshape(n // tb, tb, PEER_PICKS), wexp, tab)
    return out.reshape(n, D_MODEL)


def _gate_act_kernel(a_ref, g_ref, e_ref, o_ref):
    a = a_ref[...]
    w = g_ref[...] * (0.5 * a * (1.0 + lax.erf(a * (2.0 ** -0.5))))
    wx = _dot(w.astype(BF16), e_ref[...])
    for j in range(ROW_CHUNKS):
        o_ref[j] = wx[:, j * LANES:(j + 1) * LANES]


def _gate_act(act, g):
    n = act.shape[0]
    tm = min(1024, n)
    spec = pl.BlockSpec((tm, LANES), lambda i: (i, 0))
    expand = (jnp.arange(D_MODEL)[None, :] // ROW_CHUNKS == jnp.arange(LANES)[:, None]).astype(BF16)
    return pl.pallas_call(
        _gate_act_kernel,
        grid=(n // tm,),
        in_specs=[spec, spec, pl.BlockSpec((LANES, D_MODEL), lambda i: (0, 0))],
        out_specs=pl.BlockSpec((ROW_CHUNKS, tm, LANES), lambda i: (0, i, 0)),
        out_shape=jax.ShapeDtypeStruct((ROW_CHUNKS, n, LANES), F32),
        compiler_params=_params(("parallel",)),
        name="peer_gate_act",
    )(act, g, expand)


def _peer(hb, w_q, subkeys, u_tab, v_tab):
    n = hb.shape[0]
    qp = _matmul(hb, w_q.astype(BF16))
    idx_t, g_t = _peer_topk(qp, subkeys.astype(BF16))
    idx = idx_t.T
    g = g_t.T
    xw = lax.bitcast_convert_type(
        hb.reshape(n, ROW_WORDS, 2, LANES).transpose(0, 1, 3, 2), I32)
    act = _peer_u(idx, xw, _pack_table(u_tab))
    wexp = _gate_act(act, g).reshape(ROW_CHUNKS, n, 1, LANES)
    return _peer_v(idx, wexp, _pack_table(v_tab))


def kernel(x, mem, a_w_in, a_w_gate2, a_b_gate, a_head_g, a_w_mem_kv, a_w_out, b_w_in, b_w_mem_kv,
           b_w_out, sb_w_kv, peer_w_q, peer_subkeys, peer_u, peer_v, ln_g, ln_b):
    batch, seq, d = x.shape
    n = batch * seq
    h = x.reshape(n, d)
    memf = mem.reshape(batch * N_MEM, d)
    pad_w = 2 * GLA_DV
    s = [GLA_K, 2 * GLA_K, 2 * GLA_K + GLA_V, 2 * GLA_K + 2 * GLA_V,
         2 * GLA_K + 2 * GLA_V + GLA_GATE_RANK]

    w_in = a_w_in[0]
    w_in_r = jnp.concatenate(
        [w_in[:, :s[3]], w_in[:, s[4]:], w_in[:, s[3]:s[4]],
         jnp.zeros((d, pad_w - GLA_GATE_RANK), F32)], axis=1).astype(BF16)
    proj = _matmul(h, w_in_r)
    wg = jnp.concatenate([a_w_gate2[0], jnp.zeros((pad_w - GLA_GATE_RANK, GLA_K), F32)],
                         axis=0).astype(BF16)
    o = _gla(proj, wg, a_b_gate[0].reshape(1, GLA_K), a_head_g[0].reshape(1, GLA_V), batch, seq)
    kv_mem = _matmul(memf, a_w_mem_kv[0].astype(BF16))
    m = _mem_attn(proj, (2 * GLA_K + 2 * GLA_V) // MEM_W, kv_mem, batch, seq)
    w_out = a_w_out[0].astype(BF16)
    h, hb = _proj_ln(o, m, h, w_out[:GLA_V], w_out[GLA_V:], ln_g[0, 0].reshape(1, d),
                     ln_b[0, 0].reshape(1, d))
    ffn = _peer(hb, peer_w_q[0], peer_subkeys[0], peer_u[0], peer_v[0])
    h, hb = _res_ln(ffn, h, ln_g[0, 1].reshape(1, d), ln_b[0, 1].reshape(1, d))

    kv_sb = _matmul(hb, sb_w_kv.astype(BF16), out_dtype=BF16)
    proj = _matmul(hb, b_w_in[0].astype(BF16))
    o = _stick_breaking(proj, kv_sb, batch, seq)
    kv_mem = _matmul(memf, b_w_mem_kv[0].astype(BF16))
    m = _mem_attn(proj, SB_W // MEM_W, kv_mem, batch, seq)
    w_out = b_w_out[0].astype(BF16)
    h, hb = _proj_ln(o, m, h, w_out[:SB_W], w_out[SB_W:], ln_g[1, 0].reshape(1, d),
                     ln_b[1, 0].reshape(1, d))
    ffn = _peer(hb, peer_w_q[1], peer_subkeys[1], peer_u[1], peer_v[1])
    h, _ = _res_ln(ffn, h, ln_g[1, 1].reshape(1, d), ln_b[1, 1].reshape(1, d))
    return h.reshape(batch, seq, d)
```

```python
import functools

import jax
import jax.numpy as jnp
from jax import lax
from jax.experimental import pallas as pl
from jax.experimental.pallas import tpu as pltpu

F32 = jnp.float32
BF16 = jnp.bfloat16
I32 = jnp.int32

LANES = 128
SUBLANES = 8
VMEM_LIMIT_DEFAULT = 48 * 1024 * 1024
VMEM_LIMIT_TABLE = 56 * 1024 * 1024

D_MODEL = 1024
N_MEM = 256
GLA_HEADS = 6
GLA_DK = 64
GLA_DV = 128
GLA_K = GLA_HEADS * GLA_DK
GLA_V = GLA_HEADS * GLA_DV
GLA_GATE_RANK = 16
GLA_TAU = 16.0
GLA_CHUNK = 64
SB_HEADS = 12
SB_DIM = 64
SB_W = SB_HEADS * SB_DIM
SB_BLOCK = 128
MEM_HEADS = 4
MEM_DIM = 64
MEM_W = MEM_HEADS * MEM_DIM
PEER_HEADS = 8
PEER_KEYS = 128
PEER_EXPERTS = PEER_KEYS * PEER_KEYS
PEER_TOPK = 16
PEER_QHALF = 128
PEER_PICKS = PEER_HEADS * PEER_TOPK
DEPTH = 2
DEEPNORM_ALPHA = (2.0 * DEPTH) ** 0.25
EPS = 1e-5

ROW_CHUNKS = D_MODEL // LANES
ROW_WORDS = ROW_CHUNKS // 2

NT_DIMS = (((1,), (1,)), ((), ()))
TN_DIMS = (((0,), (0,)), ((), ()))


def _params(semantics, vmem=VMEM_LIMIT_DEFAULT):
    return pltpu.CompilerParams(dimension_semantics=semantics, vmem_limit_bytes=vmem)


def _log_sigmoid(z):
    return jnp.minimum(z, 0.0) - jnp.log1p(jnp.exp(-jnp.abs(z)))


def _dot(a, b):
    return jnp.dot(a, b, preferred_element_type=F32)


def _split_bf16(x, parts):
    out = []
    r = x
    for _ in range(parts):
        p = r.astype(BF16)
        out.append(p)
        r = r - p.astype(F32)
    return out


def _layer_norm(y, g, b):
    mu = jnp.mean(y, axis=-1, keepdims=True)
    yc = y - mu
    var = jnp.mean(yc * yc, axis=-1, keepdims=True)
    return yc * lax.rsqrt(var + EPS) * g + b


def _matmul_kernel(x_ref, w_ref, o_ref):
    o_ref[...] = _dot(x_ref[...].astype(BF16), w_ref[...]).astype(o_ref.dtype)


MATMUL_ROWS = 512


def _matmul(x, w, out_dtype=F32):
    m, k = x.shape
    n = w.shape[1]
    tm = min(MATMUL_ROWS, m)
    return pl.pallas_call(
        _matmul_kernel,
        grid=(m // tm,),
        in_specs=[pl.BlockSpec((tm, k), lambda i: (i, 0)),
                  pl.BlockSpec((k, n), lambda i: (0, 0))],
        out_specs=pl.BlockSpec((tm, n), lambda i: (i, 0)),
        out_shape=jax.ShapeDtypeStruct((m, n), out_dtype),
        compiler_params=_params(("parallel",)),
        name="matmul",
    )(x, w)


GLA_ROWS = 256


def _gla_kernel(q_ref, k_ref, v_ref, r_ref, glr_ref, wg_ref, bg_ref, hg_ref, o_ref, st_ref):
    @pl.when(pl.program_id(2) == 0)
    def _():
        st_ref[...] = jnp.zeros_like(st_ref)

    c_rows = GLA_CHUNK
    lane = lax.broadcasted_iota(I32, (c_rows, LANES), 1)
    head_masks = (lane < GLA_DK, lane >= GLA_DK)
    row = lax.broadcasted_iota(I32, (c_rows, c_rows), 0)
    col = lax.broadcasted_iota(I32, (c_rows, c_rows), 1)
    causal = row >= col
    tril = causal.astype(BF16)
    wg = wg_ref[...]
    bg = bg_ref[...]
    hg = hg_ref[...]
    chunks = range(q_ref.shape[0] // c_rows)
    heads = range(2)
    rows = [slice(c * c_rows, (c + 1) * c_rows) for c in chunks]
    cols = [slice(a * GLA_DV, (a + 1) * GLA_DV) for a in heads]
    z = [_dot(glr_ref[rows[c], :].astype(BF16), wg) + bg for c in chunks]
    lg_parts = [_split_bf16(_log_sigmoid(z[c]) * (1.0 / GLA_TAU), 3) for c in chunks]
    bc = [sum(_dot(tril, part) for part in lg_parts[c]) for c in chunks]
    bl = [bc[c][c_rows - 1:c_rows, :] for c in chunks]
    qd = [q_ref[rows[c], :] * (GLA_DK ** -0.5) * jnp.exp(bc[c]) for c in chunks]
    ki = [(k_ref[rows[c], :] * jnp.exp(-bc[c])).astype(BF16) for c in chunks]
    ke = [k_ref[rows[c], :] * jnp.exp(bl[c] - bc[c]) for c in chunks]
    dec = [jnp.exp(bl[c]) for c in chunks]
    qa = [[jnp.where(head_masks[a], qd[c], 0.0).astype(BF16) for a in heads] for c in chunks]
    kea = [[jnp.where(head_masks[a], ke[c], 0.0).astype(BF16) for a in heads] for c in chunks]
    va = [[v_ref[rows[c], cols[a]].astype(BF16) for a in heads] for c in chunks]
    attn = [[jnp.where(causal, lax.dot_general(qa[c][a], ki[c], NT_DIMS, preferred_element_type=F32),
                       0.0).astype(BF16) for a in heads] for c in chunks]
    inc = [[lax.dot_general(va[c][a], kea[c][a], TN_DIMS, preferred_element_type=F32)
            for a in heads] for c in chunks]
    o_intra = [[_dot(attn[c][a], va[c][a]) for a in heads] for c in chunks]
    states = [st_ref[0], st_ref[1]]
    for c in chunks:
        for a in heads:
            o = o_intra[c][a] + lax.dot_general(qa[c][a], states[a].astype(BF16), NT_DIMS,
                                                preferred_element_type=F32)
            states[a] = states[a] * dec[c] + inc[c][a]
            ms = jnp.mean(o * o, axis=-1, keepdims=True)
            r = r_ref[rows[c], cols[a]]
            silu = r / (1.0 + jnp.exp(-r))
            o_ref[rows[c], cols[a]] = o * lax.rsqrt(ms + EPS) * hg[:, cols[a]] * silu
    st_ref[0] = states[0]
    st_ref[1] = states[1]


def _gla(proj, wg, bg, hg, batch, seq):
    n = proj.shape[0]
    tr = min(GLA_ROWS, seq)
    nt = seq // tr
    rowmap = lambda b, j, t: b * nt + t
    pair_w = 2 * GLA_DV
    return pl.pallas_call(
        _gla_kernel,
        grid=(batch, GLA_HEADS // 2, nt),
        in_specs=[
            pl.BlockSpec((tr, LANES), lambda b, j, t: (rowmap(b, j, t), j)),
            pl.BlockSpec((tr, LANES), lambda b, j, t: (rowmap(b, j, t), GLA_K // LANES + j)),
            pl.BlockSpec((tr, pair_w), lambda b, j, t: (rowmap(b, j, t), 2 * GLA_K // pair_w + j)),
            pl.BlockSpec((tr, pair_w),
                         lambda b, j, t: (rowmap(b, j, t), (2 * GLA_K + GLA_V) // pair_w + j)),
            pl.BlockSpec((tr, pair_w),
                         lambda b, j, t: (rowmap(b, j, t), (2 * GLA_K + 2 * GLA_V + MEM_W) // pair_w)),
            pl.BlockSpec((pair_w, LANES), lambda b, j, t: (0, j)),
            pl.BlockSpec((1, LANES), lambda b, j, t: (0, j)),
            pl.BlockSpec((1, pair_w), lambda b, j, t: (0, j)),
        ],
        out_specs=pl.BlockSpec((tr, pair_w), lambda b, j, t: (rowmap(b, j, t), j)),
        out_shape=jax.ShapeDtypeStruct((n, GLA_V), F32),
        scratch_shapes=[pltpu.VMEM((2, GLA_DV, LANES), F32)],
        compiler_params=_params(("parallel", "parallel", "arbitrary")),
        name="gla",
    )(proj, proj, proj, proj, proj, wg, bg, hg)


MEM_ROWS = 512


def _mem_attn_kernel(q_ref, kv_ref, o_ref):
    tq = q_ref.shape[0]
    lane = lax.broadcasted_iota(I32, (tq, LANES), 1)
    lo_head = lane < MEM_DIM
    for j in range(MEM_W // LANES):
        cols = slice(j * LANES, (j + 1) * LANES)
        q2 = q_ref[:, cols]
        k2 = kv_ref[:, cols].astype(BF16)
        v2 = kv_ref[:, MEM_W + j * LANES:MEM_W + (j + 1) * LANES].astype(BF16)
        outs = []
        for a in range(2):
            qa = jnp.where(lo_head if a == 0 else ~lo_head, q2, 0.0).astype(BF16)
            s = lax.dot_general(qa, k2, NT_DIMS, preferred_element_type=F32) * (MEM_DIM ** -0.5)
            e = jnp.exp(s - jnp.max(s, axis=-1, keepdims=True))
            p = e / jnp.sum(e, axis=-1, keepdims=True)
            outs.append(_dot(p.astype(BF16), v2))
        o_ref[:, cols] = jnp.where(lo_head, outs[0], outs[1])


def _mem_attn(proj, q_col_block, kv, batch, seq):
    n = proj.shape[0]
    tq = min(MEM_ROWS, seq)
    nt = seq // tq
    return pl.pallas_call(
        _mem_attn_kernel,
        grid=(batch, nt),
        in_specs=[pl.BlockSpec((tq, MEM_W), lambda b, t: (b * nt + t, q_col_block)),
                  pl.BlockSpec((N_MEM, 2 * MEM_W), lambda b, t: (b, 0))],
        out_specs=pl.BlockSpec((tq, MEM_W), lambda b, t: (b * nt + t, 0)),
        out_shape=jax.ShapeDtypeStruct((n, MEM_W), F32),
        compiler_params=_params(("parallel", "parallel")),
        name="mem_attn",
    )(proj, kv)


SB_TILE = 256


SB_MASKED = -1e30


def _sb_kernel(q_ref, k_ref, v_ref, o_ref, *scratch):
    acc_refs, run_refs = scratch[0:2], scratch[2:4]
    stage = (scratch[4:10], scratch[10:16])
    i = pl.program_id(2)
    t = q_ref.shape[0]
    heads = range(2)
    row = lax.broadcasted_iota(I32, (t, t), 0)
    col = lax.broadcasted_iota(I32, (t, t), 1)
    later = (row > col).astype(BF16)
    lo_head = lax.broadcasted_iota(I32, (t, LANES), 1) < SB_DIM
    q2 = q_ref[...] * (SB_DIM ** -0.5)
    qa = (jnp.where(lo_head, q2, 0.0).astype(BF16), jnp.where(lo_head, 0.0, q2).astype(BF16))
    for ref in acc_refs + run_refs:
        ref[...] = jnp.zeros_like(ref)

    def scores(j, p, diag):
        kj = k_ref[pl.ds(pl.multiple_of(j * t, t), t), :]
        z = [lax.dot_general(qa[a], kj, NT_DIMS, preferred_element_type=F32) for a in heads]
        for a in heads:
            ls = jnp.minimum(z[a], 0.0) - jnp.log(1.0 + jnp.exp(-jnp.abs(z[a])))
            ln = ls - z[a]
            if diag:
                visible = col < row
                ln = jnp.where(visible, ln, 0.0)
                ls = jnp.where(visible, ls, SB_MASKED)
            hi, lo = _split_bf16(ln, 2)
            stage[p][a][...] = ls
            stage[p][2 + a][...] = hi
            stage[p][4 + a][...] = lo

    def apply(j, p):
        vj = v_ref[pl.ds(pl.multiple_of(j * t, t), t), :]
        hi = [stage[p][2 + a][...] for a in heads]
        lo = [stage[p][4 + a][...] for a in heads]
        exc = [_dot(hi[a], later) + _dot(lo[a], later) for a in heads]
        run = [run_refs[a][...] for a in heads]
        a_w = [jnp.exp(stage[p][a][...] + exc[a] + jnp.concatenate([run[a]] * (t // LANES), axis=1))
               for a in heads]
        out = [_dot(a_w[a].astype(BF16), vj) for a in heads]
        for a in heads:
            acc_refs[a][...] += out[a]
            total = exc[a][:, 0:1] + hi[a][:, 0:1].astype(F32) + lo[a][:, 0:1].astype(F32)
            run_refs[a][...] = run[a] + jnp.broadcast_to(total, (t, LANES))

    scores(i, 0, True)

    def two_steps(m, carry):
        j = i - 2 * m
        scores(j - 1, 1, False)
        apply(j, 0)
        scores(j - 2, 0, False)
        apply(j - 1, 1)
        return carry

    lax.fori_loop(0, i // 2, two_steps, 0)

    @pl.when(i % 2 == 1)
    def _():
        scores(0, 1, False)
        apply(1, 0)
        apply(0, 1)

    @pl.when(i % 2 == 0)
    def _():
        apply(0, 0)

    o_ref[...] = jnp.where(lo_head, acc_refs[0][...], acc_refs[1][...])


def _stick_breaking(qproj, kv, batch, seq):
    n = qproj.shape[0]
    t = min(SB_TILE, seq)
    nq = seq // t
    pairs = SB_W // LANES
    return pl.pallas_call(
        _sb_kernel,
        grid=(batch, pairs, nq),
        in_specs=[pl.BlockSpec((t, LANES), lambda b, j, i: (b * nq + i, j)),
                  pl.BlockSpec((seq, LANES), lambda b, j, i: (b, j)),
                  pl.BlockSpec((seq, LANES), lambda b, j, i: (b, pairs + j))],
        out_specs=pl.BlockSpec((t, LANES), lambda b, j, i: (b * nq + i, j)),
        out_shape=jax.ShapeDtypeStruct((n, SB_W), F32),
        scratch_shapes=([pltpu.VMEM((t, LANES), F32)] * 4
                        + ([pltpu.VMEM((t, t), F32)] * 2 + [pltpu.VMEM((t, t), BF16)] * 4) * 2),
        compiler_params=_params(("parallel", "parallel", "arbitrary")),
        name="stick_breaking",
    )(qproj, kv, kv)


LN_ROWS = 256


def _proj_ln_kernel(a1_ref, a2_ref, h_ref, w1_ref, w2_ref, g_ref, b_ref, o_ref, ob_ref):
    mix = _dot(a1_ref[...].astype(BF16), w1_ref[...]) + _dot(a2_ref[...].astype(BF16), w2_ref[...])
    y = _layer_norm(DEEPNORM_ALPHA * h_ref[...] + mix, g_ref[...], b_ref[...])
    o_ref[...] = y
    ob_ref[...] = y.astype(BF16)


def _ln_out(h, tm):
    spec = pl.BlockSpec((tm, h.shape[1]), lambda i: (i, 0))
    return [spec, spec], [jax.ShapeDtypeStruct(h.shape, F32), jax.ShapeDtypeStruct(h.shape, BF16)]


def _proj_ln(a1, a2, h, w1, w2, g, b):
    n = h.shape[0]
    tm = min(LN_ROWS, n)
    full = lambda arr: pl.BlockSpec(arr.shape, lambda i: (0, 0))
    rows = lambda arr: pl.BlockSpec((tm, arr.shape[1]), lambda i: (i, 0))
    out_specs, out_shape = _ln_out(h, tm)
    return pl.pallas_call(
        _proj_ln_kernel,
        grid=(n // tm,),
        in_specs=[rows(a1), rows(a2), rows(h), full(w1), full(w2), full(g), full(b)],
        out_specs=out_specs,
        out_shape=out_shape,
        compiler_params=_params(("parallel",)),
        name="proj_ln",
    )(a1, a2, h, w1, w2, g, b)


def _res_ln_kernel(f_ref, h_ref, g_ref, b_ref, o_ref, ob_ref):
    y = _layer_norm(DEEPNORM_ALPHA * h_ref[...] + f_ref[...], g_ref[...], b_ref[...])
    o_ref[...] = y
    ob_ref[...] = y.astype(BF16)


def _res_ln(f, h, g, b):
    n = h.shape[0]
    tm = min(512, n)
    full = lambda arr: pl.BlockSpec(arr.shape, lambda i: (0, 0))
    rows = lambda arr: pl.BlockSpec((tm, arr.shape[1]), lambda i: (i, 0))
    out_specs, out_shape = _ln_out(h, tm)
    return pl.pallas_call(
        _res_ln_kernel,
        grid=(n // tm,),
        in_specs=[rows(f), rows(h), full(g), full(b)],
        out_specs=out_specs,
        out_shape=out_shape,
        compiler_params=_params(("parallel",)),
        name="res_ln",
    )(f, h, g, b)


TOPK_TOKENS = 256


def _top16_rows(problems):
    state = []
    for vals, payload in problems:
        k = vals.shape[0]
        rows = lax.broadcasted_iota(I32, (k // 2, vals.shape[1]), 0).astype(F32)
        first, second = vals[:k // 2], vals[k // 2:]
        keep = first >= second
        slot = {"hi": jnp.where(keep, first, second), "lo": jnp.where(keep, second, first),
                "row_hi": jnp.where(keep, rows, rows + float(k // 2)),
                "row_lo": jnp.where(keep, rows + float(k // 2), rows), "k": float(k)}
        if payload is not None:
            slot["pay_hi"] = jnp.where(keep, payload[:k // 2], payload[k // 2:])
            slot["pay_lo"] = jnp.where(keep, payload[k // 2:], payload[:k // 2])
        state.append(slot)
    outs = [([], []) for _ in problems]
    for _ in range(PEER_TOPK):
        for p, slot in enumerate(state):
            m = jnp.max(slot["hi"], axis=0, keepdims=True)
            tied_rows = jnp.where(slot["hi"] == m, slot["row_hi"], slot["k"])
            am = jnp.min(tied_rows, axis=0, keepdims=True)
            hit = tied_rows == am
            outs[p][0].append(m)
            if "pay_hi" in slot:
                outs[p][1].append(jnp.max(jnp.where(hit, slot["pay_hi"], -1.0), axis=0, keepdims=True))
                slot["pay_hi"] = jnp.where(hit, slot["pay_lo"], slot["pay_hi"])
            else:
                outs[p][1].append(am)
            slot["hi"] = jnp.where(hit, slot["lo"], slot["hi"])
            slot["row_hi"] = jnp.where(hit, slot["row_lo"], slot["row_hi"])
            slot["lo"] = jnp.where(hit, -jnp.inf, slot["lo"])
    return outs


def _pair_candidates(top0, top1):
    (s0, i0), (s1, i1) = top0, top1
    s0_all = jnp.concatenate(s0, axis=0)
    i0_all = jnp.concatenate(i0, axis=0)
    s1_all = jnp.concatenate(s1, axis=0)
    i1_all = jnp.concatenate(i1, axis=0)
    half = PEER_TOPK // 2
    sub = lax.broadcasted_iota(I32, (half, s0_all.shape[1]), 0)
    cand_s = [s0[0] + s1_all]
    cand_i = [i0[0] * float(PEER_KEYS) + i1_all]
    for i in range(1, half):
        cand_s.append(jnp.where(sub < PEER_TOPK // (i + 1), s0[i] + s1_all[:half], -jnp.inf))
        cand_i.append(i0[i] * float(PEER_KEYS) + i1_all[:half])
    cand_s.append(s0_all[half:] + s1[0])
    cand_i.append(i0_all[half:] * float(PEER_KEYS) + i1[0])
    return jnp.concatenate(cand_s, axis=0), jnp.concatenate(cand_i, axis=0)


def _peer_topk_kernel(q_ref, sk_ref, idx_ref, g_ref):
    groups = q_ref.shape[0] // LANES
    scores = []
    for grp in range(groups):
        for p in range(2):
            qp = q_ref[grp * LANES:(grp + 1) * LANES, p * PEER_QHALF:(p + 1) * PEER_QHALF]
            s_t = lax.dot_general(sk_ref[0, p], qp.astype(BF16), NT_DIMS, preferred_element_type=F32)
            scores.append((s_t, None))
    tops = _top16_rows(scores)
    best = _top16_rows([_pair_candidates(tops[2 * grp], tops[2 * grp + 1]) for grp in range(groups)])
    for grp, (best_s, best_i) in enumerate(best):
        cols = slice(grp * LANES, (grp + 1) * LANES)
        e = [jnp.exp(s - best_s[0]) for s in best_s]
        g_ref[:, cols] = jnp.concatenate(e, axis=0) / sum(e)
        idx_ref[:, cols] = (jnp.concatenate(best_i, axis=0) * float(ROW_WORDS)).astype(I32)


def _peer_topk(qp, subkeys):
    n = qp.shape[0]
    tt = min(TOPK_TOKENS, n)
    spec_out = pl.BlockSpec((PEER_TOPK, tt), lambda i, h: (h, i))
    return pl.pallas_call(
        _peer_topk_kernel,
        grid=(n // tt, PEER_HEADS),
        in_specs=[pl.BlockSpec((tt, 2 * PEER_QHALF), lambda i, h: (i, h)),
                  pl.BlockSpec((1, 2, PEER_KEYS, PEER_QHALF), lambda i, h: (h, 0, 0, 0))],
        out_specs=[spec_out, spec_out],
        out_shape=[jax.ShapeDtypeStruct((PEER_PICKS, n), I32),
                   jax.ShapeDtypeStruct((PEER_PICKS, n), F32)],
        compiler_params=_params(("parallel", "parallel")),
        name="peer_topk",
    )(qp, subkeys)


def _pack_kernel(x_ref, o_ref):
    o_ref[...] = pltpu.bitcast(x_ref[...].astype(BF16), I32)


def _pack_rows(x2d):
    r = x2d.shape[0]
    tr = min(4096, r)
    return pl.pallas_call(
        _pack_kernel,
        grid=(r // tr,),
        in_specs=[pl.BlockSpec((tr, LANES), lambda i: (i, 0))],
        out_specs=pl.BlockSpec((tr // 2, LANES), lambda i: (i, 0)),
        out_shape=jax.ShapeDtypeStruct((r // 2, LANES), I32),
        compiler_params=_params(("parallel",)),
        name="pack_bf16",
    )(x2d)


def _pack_table(tab):
    e = tab.shape[0]
    return _pack_rows(tab.reshape(e * ROW_CHUNKS, LANES))


PEER_TOKENS = 64
PEER_UNROLL = 32
PAIRS = PEER_PICKS // 2


def _gather_rows(idx_ref, tab_ref, n):
    pieces = []
    for k in range(PAIRS):
        e0 = idx_ref[0, n, 2 * k]
        e1 = idx_ref[0, n, 2 * k + 1]
        words = jnp.concatenate(
            [tab_ref[pl.ds(pl.multiple_of(e, ROW_WORDS), ROW_WORDS), :] for e in (e0, e1)],
            axis=0)
        pieces.append(pltpu.bitcast(words, BF16))
    return pieces


def _token_loop(n_tokens, token):
    def body(m, carry):
        for u in range(PEER_UNROLL):
            token(m * PEER_UNROLL + u)
        return carry

    lax.fori_loop(0, n_tokens // PEER_UNROLL, body, 0)


def _peer_u_kernel(idx_ref, x_ref, tab_ref, o_ref):
    ones = jnp.ones((LANES, LANES), BF16)
    lane = lax.broadcasted_iota(I32, (ROW_CHUNKS, LANES), 1)

    def token(n):
        xw = pltpu.bitcast(jnp.concatenate([x_ref[n], x_ref[n]], axis=0), BF16)
        prods = [piece * xw for piece in _gather_rows(idx_ref, tab_ref, n)]
        z = _dot(jnp.concatenate(prods, axis=0), ones)
        n_acc = 4
        acc = [jnp.zeros((ROW_CHUNKS, LANES), F32) for _ in range(n_acc)]
        for e in range(PEER_PICKS):
            acc[e % n_acc] = acc[e % n_acc] + jnp.where(
                lane == e, z[e * ROW_CHUNKS:(e + 1) * ROW_CHUNKS, :], 0.0)
        o_ref[n] = jnp.sum(sum(acc), axis=0, keepdims=True)

    _token_loop(o_ref.shape[0], token)


def _peer_v_kernel(idx_ref, wexp_ref, tab_ref, o_ref):
    lane = lax.broadcasted_iota(I32, (ROW_CHUNKS, LANES), 1)
    sub = lax.broadcasted_iota(I32, (ROW_CHUNKS, LANES), 0)
    own_chunk = (lane & (ROW_CHUNKS - 1)) == sub

    def token(n):
        rows = jnp.concatenate(_gather_rows(idx_ref, tab_ref, n), axis=0)
        lhs = jnp.concatenate(
            [jnp.where(own_chunk, jnp.broadcast_to(wexp_ref[j, n], (ROW_CHUNKS, LANES)), 0.0)
             for j in range(ROW_CHUNKS)], axis=1).astype(BF16)
        o_ref[n] = _dot(lhs, rows)

    _token_loop(o_ref.shape[0], token)


def _idx_spec(tb):
    return pl.BlockSpec((1, tb, PEER_PICKS), lambda i: (i, 0, 0), memory_space=pltpu.SMEM)


def _table_spec(tab):
    return pl.BlockSpec(tab.shape, lambda i: (0, 0), pipeline_mode=pl.Buffered(1))


def _peer_u(idx, xw, tab):
    n = idx.shape[0]
    tb = min(PEER_TOKENS, n)
    out = pl.pallas_call(
        _peer_u_kernel,
        grid=(n // tb,),
        in_specs=[_idx_spec(tb), pl.BlockSpec((tb, ROW_WORDS, LANES), lambda i: (i, 0, 0)),
                  _table_spec(tab)],
        out_specs=pl.BlockSpec((tb, 1, LANES), lambda i: (i, 0, 0)),
        out_shape=jax.ShapeDtypeStruct((n, 1, LANES), F32),
        compiler_params=_params(("arbitrary",), VMEM_LIMIT_TABLE),
        name="peer_u",
    )(idx.reshape(n // tb, tb, PEER_PICKS), xw, tab)
    return out.reshape(n, LANES)


def _peer_v(idx, wexp, tab):
    n = idx.shape[0]
    tb = min(PEER_TOKENS, n)
    out = pl.pallas_call(
        _peer_v_kernel,
        grid=(n // tb,),
        in_specs=[_idx_spec(tb),
                  pl.BlockSpec((ROW_CHUNKS, tb, 1, LANES), lambda i: (0, i, 0, 0)),
                  _table_spec(tab)],
        out_specs=pl.BlockSpec((tb, ROW_CHUNKS, LANES), lambda i: (i, 0, 0)),
        out_shape=jax.ShapeDtypeStruct((n, ROW_CHUNKS, LANES), F32),
        compiler_params=_params(("arbitrary",), VMEM_LIMIT_TABLE),
        name="peer_v",
    )(idx.reshape(n // tb, tb, PEER_PICKS), wexp, tab)
    return out.reshape(n, D_MODEL)


def _gate_act_kernel(a_ref, g_ref, e_ref, o_ref):
    a = a_ref[...]
    w = g_ref[...] * (0.5 * a * (1.0 + lax.erf(a * (2.0 ** -0.5))))
    wx = _dot(w.astype(BF16), e_ref[...])
    for j in range(ROW_CHUNKS):
        o_ref[j] = wx[:, j * LANES:(j + 1) * LANES]


def _gate_act(act, g):
    n = act.shape[0]
    tm = min(1024, n)
    spec = pl.BlockSpec((tm, LANES), lambda i: (i, 0))
    expand = (jnp.arange(D_MODEL)[None, :] // ROW_CHUNKS == jnp.arange(LANES)[:, None]).astype(BF16)
    return pl.pallas_call(
        _gate_act_kernel,
        grid=(n // tm,),
        in_specs=[spec, spec, pl.BlockSpec((LANES, D_MODEL), lambda i: (0, 0))],
        out_specs=pl.BlockSpec((ROW_CHUNKS, tm, LANES), lambda i: (0, i, 0)),
        out_shape=jax.ShapeDtypeStruct((ROW_CHUNKS, n, LANES), F32),
        compiler_params=_params(("parallel",)),
        name="peer_gate_act",
    )(act, g, expand)


def _peer(hb, w_q, subkeys, u_tab, v_tab):
    n = hb.shape[0]
    qp = _matmul(hb, w_q.astype(BF16))
    idx_t, g_t = _peer_topk(qp, subkeys.astype(BF16))
    idx = idx_t.T
    g = g_t.T
    xw = lax.bitcast_convert_type(
        hb.reshape(n, ROW_WORDS, 2, LANES).transpose(0, 1, 3, 2), I32)
    act = _peer_u(idx, xw, _pack_table(u_tab))
    wexp = _gate_act(act, g).reshape(ROW_CHUNKS, n, 1, LANES)
    return _peer_v(idx, wexp, _pack_table(v_tab))


def kernel(x, mem, a_w_in, a_w_gate2, a_b_gate, a_head_g, a_w_mem_kv, a_w_out, b_w_in, b_w_mem_kv,
           b_w_out, sb_w_kv, peer_w_q, peer_subkeys, peer_u, peer_v, ln_g, ln_b):
    batch, seq, d = x.shape
    n = batch * seq
    h = x.reshape(n, d)
    memf = mem.reshape(batch * N_MEM, d)
    pad_w = 2 * GLA_DV
    s = [GLA_K, 2 * GLA_K, 2 * GLA_K + GLA_V, 2 * GLA_K + 2 * GLA_V,
         2 * GLA_K + 2 * GLA_V + GLA_GATE_RANK]

    w_in = a_w_in[0]
    w_in_r = jnp.concatenate(
        [w_in[:, :s[3]], w_in[:, s[4]:], w_in[:, s[3]:s[4]],
         jnp.zeros((d, pad_w - GLA_GATE_RANK), F32)], axis=1).astype(BF16)
    proj = _matmul(h, w_in_r)
    wg = jnp.concatenate([a_w_gate2[0], jnp.zeros((pad_w - GLA_GATE_RANK, GLA_K), F32)],
                         axis=0).astype(BF16)
    o = _gla(proj, wg, a_b_gate[0].reshape(1, GLA_K), a_head_g[0].reshape(1, GLA_V), batch, seq)
    kv_mem = _matmul(memf, a_w_mem_kv[0].astype(BF16))
    m = _mem_attn(proj, (2 * GLA_K + 2 * GLA_V) // MEM_W, kv_mem, batch, seq)
    w_out = a_w_out[0].astype(BF16)
    h, hb = _proj_ln(o, m, h, w_out[:GLA_V], w_out[GLA_V:], ln_g[0, 0].reshape(1, d),
                     ln_b[0, 0].reshape(1, d))
    ffn = _peer(hb, peer_w_q[0], peer_subkeys[0], peer_u[0], peer_v[0])
    h, hb = _res_ln(ffn, h, ln_g[0, 1].reshape(1, d), ln_b[0, 1].reshape(1, d))

    kv_sb = _matmul(hb, sb_w_kv.astype(BF16), out_dtype=BF16)
    proj = _matmul(hb, b_w_in[0].astype(BF16))
    o = _stick_breaking(proj, kv_sb, batch, seq)
    kv_mem = _matmul(memf, b_w_mem_kv[0].astype(BF16))
    m = _mem_attn(proj, SB_W // MEM_W, kv_mem, batch, seq)
    w_out = b_w_out[0].astype(BF16)
    h, hb = _proj_ln(o, m, h, w_out[:SB_W], w_out[SB_W:], ln_g[1, 0].reshape(1, d),
                     ln_b[1, 0].reshape(1, d))
    ffn = _peer(hb, peer_w_q[1], peer_subkeys[1], peer_u[1], peer_v[1])
    h, _ = _res_ln(ffn, h, ln_g[1, 1].reshape(1, d), ln_b[1, 1].reshape(1, d))
    return h.reshape(batch, seq, d)
```

```python
import functools

import jax
import jax.numpy as jnp
from jax import lax
from jax.experimental import pallas as pl
from jax.experimental.pallas import tpu as pltpu

F32 = jnp.float32
BF16 = jnp.bfloat16
I32 = jnp.int32

LANES = 128
SUBLANES = 8
VMEM_LIMIT_DEFAULT = 48 * 1024 * 1024
VMEM_LIMIT_TABLE = 56 * 1024 * 1024

D_MODEL = 1024
N_MEM = 256
GLA_HEADS = 6
GLA_DK = 64
GLA_DV = 128
GLA_K = GLA_HEADS * GLA_DK
GLA_V = GLA_HEADS * GLA_DV
GLA_GATE_RANK = 16
GLA_TAU = 16.0
GLA_CHUNK = 64
SB_HEADS = 12
SB_DIM = 64
SB_W = SB_HEADS * SB_DIM
SB_BLOCK = 128
MEM_HEADS = 4
MEM_DIM = 64
MEM_W = MEM_HEADS * MEM_DIM
PEER_HEADS = 8
PEER_KEYS = 128
PEER_EXPERTS = PEER_KEYS * PEER_KEYS
PEER_TOPK = 16
PEER_QHALF = 128
PEER_PICKS = PEER_HEADS * PEER_TOPK
DEPTH = 2
DEEPNORM_ALPHA = (2.0 * DEPTH) ** 0.25
EPS = 1e-5

ROW_CHUNKS = D_MODEL // LANES
ROW_WORDS = ROW_CHUNKS // 2

NT_DIMS = (((1,), (1,)), ((), ()))
TN_DIMS = (((0,), (0,)), ((), ()))


def _params(semantics, vmem=VMEM_LIMIT_DEFAULT):
    return pltpu.CompilerParams(dimension_semantics=semantics, vmem_limit_bytes=vmem)


def _log_sigmoid(z):
    return jnp.minimum(z, 0.0) - jnp.log1p(jnp.exp(-jnp.abs(z)))


def _dot(a, b):
    return jnp.dot(a, b, preferred_element_type=F32)


def _split_bf16(x, parts):
    out = []
    r = x
    for _ in range(parts):
        p = r.astype(BF16)
        out.append(p)
        r = r - p.astype(F32)
    return out


def _layer_norm(y, g, b):
    mu = jnp.mean(y, axis=-1, keepdims=True)
    yc = y - mu
    var = jnp.mean(yc * yc, axis=-1, keepdims=True)
    return yc * lax.rsqrt(var + EPS) * g + b


def _matmul_kernel(x_ref, w_ref, o_ref):
    o_ref[...] = _dot(x_ref[...].astype(BF16), w_ref[...]).astype(o_ref.dtype)


MATMUL_ROWS = 512


def _matmul(x, w, out_dtype=F32):
    m, k = x.shape
    n = w.shape[1]
    tm = min(MATMUL_ROWS, m)
    return pl.pallas_call(
        _matmul_kernel,
        grid=(m // tm,),
        in_specs=[pl.BlockSpec((tm, k), lambda i: (i, 0)),
                  pl.BlockSpec((k, n), lambda i: (0, 0))],
        out_specs=pl.BlockSpec((tm, n), lambda i: (i, 0)),
        out_shape=jax.ShapeDtypeStruct((m, n), out_dtype),
        compiler_params=_params(("parallel",)),
        name="matmul",
    )(x, w)


GLA_ROWS = 256


def _gla_kernel(q_ref, k_ref, v_ref, r_ref, glr_ref, wg_ref, bg_ref, hg_ref, o_ref, st_ref):
    @pl.when(pl.program_id(2) == 0)
    def _():
        st_ref[...] = jnp.zeros_like(st_ref)

    c_rows = GLA_CHUNK
    lane = lax.broadcasted_iota(I32, (c_rows, LANES), 1)
    head_masks = (lane < GLA_DK, lane >= GLA_DK)
    row = lax.broadcasted_iota(I32, (c_rows, c_rows), 0)
    col = lax.broadcasted_iota(I32, (c_rows, c_rows), 1)
    causal = row >= col
    tril = causal.astype(BF16)
    wg = wg_ref[...]
    bg = bg_ref[...]
    hg = hg_ref[...]
    chunks = range(q_ref.shape[0] // c_rows)
    heads = range(2)
    rows = [slice(c * c_rows, (c + 1) * c_rows) for c in chunks]
    cols = [slice(a * GLA_DV, (a + 1) * GLA_DV) for a in heads]
    z = [_dot(glr_ref[rows[c], :].astype(BF16), wg) + bg for c in chunks]
    lg_parts = [_split_bf16(_log_sigmoid(z[c]) * (1.0 / GLA_TAU), 3) for c in chunks]
    bc = [sum(_dot(tril, part) for part in lg_parts[c]) for c in chunks]
    bl = [bc[c][c_rows - 1:c_rows, :] for c in chunks]
    qd = [q_ref[rows[c], :] * (GLA_DK ** -0.5) * jnp.exp(bc[c]) for c in chunks]
    ki = [(k_ref[rows[c], :] * jnp.exp(-bc[c])).astype(BF16) for c in chunks]
    ke = [k_ref[rows[c], :] * jnp.exp(bl[c] - bc[c]) for c in chunks]
    dec = [jnp.exp(bl[c]) for c in chunks]
    qa = [[jnp.where(head_masks[a], qd[c], 0.0).astype(BF16) for a in heads] for c in chunks]
    kea = [[jnp.where(head_masks[a], ke[c], 0.0).astype(BF16) for a in heads] for c in chunks]
    va = [[v_ref[rows[c], cols[a]].astype(BF16) for a in heads] for c in chunks]
    attn = [[jnp.where(causal, lax.dot_general(qa[c][a], ki[c], NT_DIMS, preferred_element_type=F32),
                       0.0).astype(BF16) for a in heads] for c in chunks]
    inc = [[lax.dot_general(va[c][a], kea[c][a], TN_DIMS, preferred_element_type=F32)
            for a in heads] for c in chunks]
    o_intra = [[_dot(attn[c][a], va[c][a]) for a in heads] for c in chunks]
    states = [st_ref[0], st_ref[1]]
    for c in chunks:
        for a in heads:
            o = o_intra[c][a] + lax.dot_general(qa[c][a], states[a].astype(BF16), NT_DIMS,
                                                preferred_element_type=F32)
            states[a] = states[a] * dec[c] + inc[c][a]
            ms = jnp.mean(o * o, axis=-1, keepdims=True)
            r = r_ref[rows[c], cols[a]]
            silu = r / (1.0 + jnp.exp(-r))
            o_ref[rows[c], cols[a]] = o * lax.rsqrt(ms + EPS) * hg[:, cols[a]] * silu
    st_ref[0] = states[0]
    st_ref[1] = states[1]


def _gla(proj, wg, bg, hg, batch, seq):
    n = proj.shape[0]
    tr = min(GLA_ROWS, seq)
    nt = seq // tr
    rowmap = lambda b, j, t: b * nt + t
    pair_w = 2 * GLA_DV
    return pl.pallas_call(
        _gla_kernel,
        grid=(batch, GLA_HEADS // 2, nt),
        in_specs=[
            pl.BlockSpec((tr, LANES), lambda b, j, t: (rowmap(b, j, t), j)),
            pl.BlockSpec((tr, LANES), lambda b, j, t: (rowmap(b, j, t), GLA_K // LANES + j)),
            pl.BlockSpec((tr, pair_w), lambda b, j, t: (rowmap(b, j, t), 2 * GLA_K // pair_w + j)),
            pl.BlockSpec((tr, pair_w),
                         lambda b, j, t: (rowmap(b, j, t), (2 * GLA_K + GLA_V) // pair_w + j)),
            pl.BlockSpec((tr, pair_w),
                         lambda b, j, t: (rowmap(b, j, t), (2 * GLA_K + 2 * GLA_V + MEM_W) // pair_w)),
            pl.BlockSpec((pair_w, LANES), lambda b, j, t: (0, j)),
            pl.BlockSpec((1, LANES), lambda b, j, t: (0, j)),
            pl.BlockSpec((1, pair_w), lambda b, j, t: (0, j)),
        ],
        out_specs=pl.BlockSpec((tr, pair_w), lambda b, j, t: (rowmap(b, j, t), j)),
        out_shape=jax.ShapeDtypeStruct((n, GLA_V), F32),
        scratch_shapes=[pltpu.VMEM((2, GLA_DV, LANES), F32)],
        compiler_params=_params(("parallel", "parallel", "arbitrary")),
        name="gla",
    )(proj, proj, proj, proj, proj, wg, bg, hg)


MEM_ROWS = 512


def _mem_attn_kernel(q_ref, kv_ref, o_ref):
    tq = q_ref.shape[0]
    lane = lax.broadcasted_iota(I32, (tq, LANES), 1)
    lo_head = lane < MEM_DIM
    for j in range(MEM_W // LANES):
        cols = slice(j * LANES, (j + 1) * LANES)
        q2 = q_ref[:, cols]
        k2 = kv_ref[:, cols].astype(BF16)
        v2 = kv_ref[:, MEM_W + j * LANES:MEM_W + (j + 1) * LANES].astype(BF16)
        outs = []
        for a in range(2):
            qa = jnp.where(lo_head if a == 0 else ~lo_head, q2, 0.0).astype(BF16)
            s = lax.dot_general(qa, k2, NT_DIMS, preferred_element_type=F32) * (MEM_DIM ** -0.5)
            e = jnp.exp(s - jnp.max(s, axis=-1, keepdims=True))
            p = e / jnp.sum(e, axis=-1, keepdims=True)
            outs.append(_dot(p.astype(BF16), v2))
        o_ref[:, cols] = jnp.where(lo_head, outs[0], outs[1])


def _mem_attn(proj, q_col_block, kv, batch, seq):
    n = proj.shape[0]
    tq = min(MEM_ROWS, seq)
    nt = seq // tq
    return pl.pallas_call(
        _mem_attn_kernel,
        grid=(batch, nt),
        in_specs=[pl.BlockSpec((tq, MEM_W), lambda b, t: (b * nt + t, q_col_block)),
                  pl.BlockSpec((N_MEM, 2 * MEM_W), lambda b, t: (b, 0))],
        out_specs=pl.BlockSpec((tq, MEM_W), lambda b, t: (b * nt + t, 0)),
        out_shape=jax.ShapeDtypeStruct((n, MEM_W), F32),
        compiler_params=_params(("parallel", "parallel")),
        name="mem_attn",
    )(proj, kv)


SB_TILE = 256


SB_MASKED = -1e30


def _sb_kernel(q_ref, k_ref, v_ref, o_ref, *scratch):
    acc_refs, run_refs = scratch[0:2], scratch[2:4]
    stage = (scratch[4:10], scratch[10:16])
    i = pl.program_id(2)
    t = q_ref.shape[0]
    heads = range(2)
    row = lax.broadcasted_iota(I32, (t, t), 0)
    col = lax.broadcasted_iota(I32, (t, t), 1)
    later = (row > col).astype(BF16)
    lo_head = lax.broadcasted_iota(I32, (t, LANES), 1) < SB_DIM
    q2 = q_ref[...] * (SB_DIM ** -0.5)
    qa = (jnp.where(lo_head, q2, 0.0).astype(BF16), jnp.where(lo_head, 0.0, q2).astype(BF16))
    for ref in acc_refs + run_refs:
        ref[...] = jnp.zeros_like(ref)

    def scores(j, p, diag):
        kj = k_ref[pl.ds(pl.multiple_of(j * t, t), t), :]
        z = [lax.dot_general(qa[a], kj, NT_DIMS, preferred_element_type=F32) for a in heads]
        for a in heads:
            ls = jnp.minimum(z[a], 0.0) - jnp.log(1.0 + jnp.exp(-jnp.abs(z[a])))
            ln = ls - z[a]
            if diag:
                visible = col < row
                ln = jnp.where(visible, ln, 0.0)
                ls = jnp.where(visible, ls, SB_MASKED)
            hi, lo = _split_bf16(ln, 2)
            stage[p][a][...] = ls
            stage[p][2 + a][...] = hi
            stage[p][4 + a][...] = lo

    def apply(j, p):
        vj = v_ref[pl.ds(pl.multiple_of(j * t, t), t), :]
        hi = [stage[p][2 + a][...] for a in heads]
        lo = [stage[p][4 + a][...] for a in heads]
        exc = [_dot(hi[a], later) + _dot(lo[a], later) for a in heads]
        run = [run_refs[a][...] for a in heads]
        a_w = [jnp.exp(stage[p][a][...] + exc[a] + jnp.concatenate([run[a]] * (t // LANES), axis=1))
               for a in heads]
        out = [_dot(a_w[a].astype(BF16), vj) for a in heads]
        for a in heads:
            acc_refs[a][...] += out[a]
            total = exc[a][:, 0:1] + hi[a][:, 0:1].astype(F32) + lo[a][:, 0:1].astype(F32)
            run_refs[a][...] = run[a] + jnp.broadcast_to(total, (t, LANES))

    scores(i, 0, True)

    def two_steps(m, carry):
        j = i - 2 * m
        scores(j - 1, 1, False)
        apply(j, 0)
        scores(j - 2, 0, False)
        apply(j - 1, 1)
        return carry

    lax.fori_loop(0, i // 2, two_steps, 0)

    @pl.when(i % 2 == 1)
    def _():
        scores(0, 1, False)
        apply(1, 0)
        apply(0, 1)

    @pl.when(i % 2 == 0)
    def _():
        apply(0, 0)

    o_ref[...] = jnp.where(lo_head, acc_refs[0][...], acc_refs[1][...])


def _stick_breaking(qproj, kv, batch, seq):
    n = qproj.shape[0]
    t = min(SB_TILE, seq)
    nq = seq // t
    pairs = SB_W // LANES
    return pl.pallas_call(
        _sb_kernel,
        grid=(batch, pairs, nq),
        in_specs=[pl.BlockSpec((t, LANES), lambda b, j, i: (b * nq + i, j)),
                  pl.BlockSpec((seq, LANES), lambda b, j, i: (b, j)),
                  pl.BlockSpec((seq, LANES), lambda b, j, i: (b, pairs + j))],
        out_specs=pl.BlockSpec((t, LANES), lambda b, j, i: (b * nq + i, j)),
        out_shape=jax.ShapeDtypeStruct((n, SB_W), F32),
        scratch_shapes=([pltpu.VMEM((t, LANES), F32)] * 4
                        + ([pltpu.VMEM((t, t), F32)] * 2 + [pltpu.VMEM((t, t), BF16)] * 4) * 2),
        compiler_params=_params(("parallel", "parallel", "arbitrary")),
        name="stick_breaking",
    )(qproj, kv, kv)


LN_ROWS = 256


def _proj_ln_kernel(a1_ref, a2_ref, h_ref, w1_ref, w2_ref, g_ref, b_ref, o_ref, ob_ref, xw_ref):
    mix = _dot(a1_ref[...].astype(BF16), w1_ref[...]) + _dot(a2_ref[...].astype(BF16), w2_ref[...])
    y = _layer_norm(DEEPNORM_ALPHA * h_ref[...] + mix, g_ref[...], b_ref[...])
    o_ref[...] = y
    ob_ref[...] = y.astype(BF16)
    tm = y.shape[0]
    bits = lax.bitcast_convert_type(y.astype(BF16).astype(F32), I32)
    for s in range(ROW_WORDS):
        lo = lax.shift_right_logical(bits[:, (2 * s) * LANES:(2 * s + 1) * LANES], 16)
        hi = bits[:, (2 * s + 1) * LANES:(2 * s + 2) * LANES]
        xw_ref[pl.ds(s, tm, stride=ROW_WORDS), :] = hi | lo


def _ln_out(h, tm):
    spec = pl.BlockSpec((tm, h.shape[1]), lambda i: (i, 0))
    return [spec, spec], [jax.ShapeDtypeStruct(h.shape, F32), jax.ShapeDtypeStruct(h.shape, BF16)]


def _proj_ln(a1, a2, h, w1, w2, g, b):
    n = h.shape[0]
    tm = min(LN_ROWS, n)
    full = lambda arr: pl.BlockSpec(arr.shape, lambda i: (0, 0))
    rows = lambda arr: pl.BlockSpec((tm, arr.shape[1]), lambda i: (i, 0))
    out_specs, out_shape = _ln_out(h, tm)
    out_specs = out_specs + [pl.BlockSpec((tm * ROW_WORDS, LANES), lambda i: (i, 0))]
    out_shape = out_shape + [jax.ShapeDtypeStruct((n * ROW_WORDS, LANES), I32)]
    return pl.pallas_call(
        _proj_ln_kernel,
        grid=(n // tm,),
        in_specs=[rows(a1), rows(a2), rows(h), full(w1), full(w2), full(g), full(b)],
        out_specs=out_specs,
        out_shape=out_shape,
        compiler_params=_params(("parallel",)),
        name="proj_ln",
    )(a1, a2, h, w1, w2, g, b)


def _res_ln_kernel(f_ref, h_ref, g_ref, b_ref, o_ref, ob_ref):
    tm = h_ref.shape[0]
    f = jnp.concatenate([f_ref[pl.ds(c, tm, stride=ROW_CHUNKS), :] for c in range(ROW_CHUNKS)], axis=1)
    y = _layer_norm(DEEPNORM_ALPHA * h_ref[...] + f, g_ref[...], b_ref[...])
    o_ref[...] = y
    ob_ref[...] = y.astype(BF16)


def _res_ln(f, h, g, b):
    n = h.shape[0]
    tm = min(512, n)
    full = lambda arr: pl.BlockSpec(arr.shape, lambda i: (0, 0))
    rows = lambda arr: pl.BlockSpec((tm, arr.shape[1]), lambda i: (i, 0))
    out_specs, out_shape = _ln_out(h, tm)
    return pl.pallas_call(
        _res_ln_kernel,
        grid=(n // tm,),
        in_specs=[pl.BlockSpec((tm * ROW_CHUNKS, LANES), lambda i: (i, 0)), rows(h), full(g), full(b)],
        out_specs=out_specs,
        out_shape=out_shape,
        compiler_params=_params(("parallel",)),
        name="res_ln",
    )(f, h, g, b)


TOPK_TOKENS = 256


def _top16_rows(problems):
    state = []
    for vals, payload in problems:
        k = vals.shape[0]
        rows = lax.broadcasted_iota(I32, (k // 2, vals.shape[1]), 0).astype(F32)
        first, second = vals[:k // 2], vals[k // 2:]
        keep = first >= second
        slot = {"hi": jnp.where(keep, first, second), "lo": jnp.where(keep, second, first),
                "row_hi": jnp.where(keep, rows, rows + float(k // 2)),
                "row_lo": jnp.where(keep, rows + float(k // 2), rows), "k": float(k)}
        if payload is not None:
            slot["pay_hi"] = jnp.where(keep, payload[:k // 2], payload[k // 2:])
            slot["pay_lo"] = jnp.where(keep, payload[k // 2:], payload[:k // 2])
        state.append(slot)
    outs = [([], []) for _ in problems]
    for _ in range(PEER_TOPK):
        for p, slot in enumerate(state):
            m = jnp.max(slot["hi"], axis=0, keepdims=True)
            tied_rows = jnp.where(slot["hi"] == m, slot["row_hi"], slot["k"])
            am = jnp.min(tied_rows, axis=0, keepdims=True)
            hit = tied_rows == am
            outs[p][0].append(m)
            if "pay_hi" in slot:
                outs[p][1].append(jnp.max(jnp.where(hit, slot["pay_hi"], -1.0), axis=0, keepdims=True))
                slot["pay_hi"] = jnp.where(hit, slot["pay_lo"], slot["pay_hi"])
            else:
                outs[p][1].append(am)
            slot["hi"] = jnp.where(hit, slot["lo"], slot["hi"])
            slot["row_hi"] = jnp.where(hit, slot["row_lo"], slot["row_hi"])
            slot["lo"] = jnp.where(hit, -jnp.inf, slot["lo"])
    return outs


def _pair_candidates(top0, top1):
    (s0, i0), (s1, i1) = top0, top1
    s0_all = jnp.concatenate(s0, axis=0)
    i0_all = jnp.concatenate(i0, axis=0)
    s1_all = jnp.concatenate(s1, axis=0)
    i1_all = jnp.concatenate(i1, axis=0)
    half = PEER_TOPK // 2
    sub = lax.broadcasted_iota(I32, (half, s0_all.shape[1]), 0)
    cand_s = [s0[0] + s1_all]
    cand_i = [i0[0] * float(PEER_KEYS) + i1_all]
    for i in range(1, half):
        cand_s.append(jnp.where(sub < PEER_TOPK // (i + 1), s0[i] + s1_all[:half], -jnp.inf))
        cand_i.append(i0[i] * float(PEER_KEYS) + i1_all[:half])
    cand_s.append(s0_all[half:] + s1[0])
    cand_i.append(i0_all[half:] * float(PEER_KEYS) + i1[0])
    return jnp.concatenate(cand_s, axis=0), jnp.concatenate(cand_i, axis=0)


def _peer_topk_kernel(q_ref, sk_ref, idx_ref, g_ref):
    groups = q_ref.shape[0] // LANES
    scores = []
    for grp in range(groups):
        for p in range(2):
            qp = q_ref[grp * LANES:(grp + 1) * LANES, p * PEER_QHALF:(p + 1) * PEER_QHALF]
            s_t = lax.dot_general(sk_ref[0, p], qp.astype(BF16), NT_DIMS, preferred_element_type=F32)
            scores.append((s_t, None))
    tops = _top16_rows(scores)
    best = _top16_rows([_pair_candidates(tops[2 * grp], tops[2 * grp + 1]) for grp in range(groups)])
    for grp, (best_s, best_i) in enumerate(best):
        cols = slice(grp * LANES, (grp + 1) * LANES)
        e = [jnp.exp(s - best_s[0]) for s in best_s]
        g_ref[:, cols] = jnp.concatenate(e, axis=0) / sum(e)
        idx_ref[:, cols] = (jnp.concatenate(best_i, axis=0) * float(ROW_WORDS)).astype(I32)


def _peer_topk(qp, subkeys):
    n = qp.shape[0]
    tt = min(TOPK_TOKENS, n)
    spec_out = pl.BlockSpec((PEER_TOPK, tt), lambda i, h: (h, i))
    return pl.pallas_call(
        _peer_topk_kernel,
        grid=(n // tt, PEER_HEADS),
        in_specs=[pl.BlockSpec((tt, 2 * PEER_QHALF), lambda i, h: (i, h)),
                  pl.BlockSpec((1, 2, PEER_KEYS, PEER_QHALF), lambda i, h: (h, 0, 0, 0))],
        out_specs=[spec_out, spec_out],
        out_shape=[jax.ShapeDtypeStruct((PEER_PICKS, n), I32),
                   jax.ShapeDtypeStruct((PEER_PICKS, n), F32)],
        compiler_params=_params(("parallel", "parallel")),
        name="peer_topk",
    )(qp, subkeys)


def _pack_kernel(x_ref, o_ref):
    o_ref[...] = pltpu.bitcast(x_ref[...].astype(BF16), I32)


def _pack_rows(x2d):
    r = x2d.shape[0]
    tr = min(4096, r)
    return pl.pallas_call(
        _pack_kernel,
        grid=(r // tr,),
        in_specs=[pl.BlockSpec((tr, LANES), lambda i: (i, 0))],
        out_specs=pl.BlockSpec((tr // 2, LANES), lambda i: (i, 0)),
        out_shape=jax.ShapeDtypeStruct((r // 2, LANES), I32),
        compiler_params=_params(("parallel",)),
        name="pack_bf16",
    )(x2d)


def _pack_table(tab):
    e = tab.shape[0]
    return _pack_rows(tab.reshape(e * ROW_CHUNKS, LANES))


PEER_TOKENS = 128
PEER_UNROLL = 32
PAIRS = PEER_PICKS // 2


def _gather_rows(idx_ref, tab_ref, n):
    pieces = []
    for k in range(PAIRS):
        e0 = idx_ref[0, n, 2 * k]
        e1 = idx_ref[0, n, 2 * k + 1]
        words = jnp.concatenate(
            [tab_ref[pl.ds(pl.multiple_of(e, ROW_WORDS), ROW_WORDS), :] for e in (e0, e1)],
            axis=0)
        pieces.append(pltpu.bitcast(words, BF16))
    return pieces


def _token_loop(n_tokens, group):
    def body(m, carry):
        for q in range(PEER_UNROLL // SUBLANES):
            group(pl.multiple_of(m * PEER_UNROLL + q * SUBLANES, SUBLANES))
        return carry

    lax.fori_loop(0, n_tokens // PEER_UNROLL, body, 0)


def _peer_u_kernel(idx_ref, x_ref, tab_ref, o_ref):
    ones = jnp.ones((LANES, LANES), BF16)
    lane = lax.broadcasted_iota(I32, (ROW_CHUNKS, LANES), 1)

    def token(n):
        x4 = x_ref[pl.ds(pl.multiple_of(n * ROW_WORDS, ROW_WORDS), ROW_WORDS), :]
        xw = pltpu.bitcast(jnp.concatenate([x4, x4], axis=0), BF16)
        prods = [piece * xw for piece in _gather_rows(idx_ref, tab_ref, n)]
        z = _dot(jnp.concatenate(prods, axis=0), ones)
        n_acc = 4
        acc = [jnp.zeros((ROW_CHUNKS, LANES), F32) for _ in range(n_acc)]
        for e in range(PEER_PICKS):
            acc[e % n_acc] = acc[e % n_acc] + jnp.where(
                lane == e, z[e * ROW_CHUNKS:(e + 1) * ROW_CHUNKS, :], 0.0)
        return jnp.sum(sum(acc), axis=0, keepdims=True)

    def group(base):
        o_ref[pl.ds(base, SUBLANES), :] = jnp.concatenate(
            [token(base + r) for r in range(SUBLANES)], axis=0)

    _token_loop(o_ref.shape[0], group)


def _peer_v_kernel(idx_ref, wexp_ref, tab_ref, o_ref):
    lane = lax.broadcasted_iota(I32, (ROW_CHUNKS, LANES), 1)
    sub = lax.broadcasted_iota(I32, (ROW_CHUNKS, LANES), 0)
    own_chunk = (lane & (ROW_CHUNKS - 1)) == sub

    def group(base):
        weights = [wexp_ref[j, pl.ds(base, SUBLANES), :] for j in range(ROW_CHUNKS)]
        for r in range(SUBLANES):
            n = base + r
            rows = jnp.concatenate(_gather_rows(idx_ref, tab_ref, n), axis=0)
            lhs = jnp.concatenate(
                [jnp.where(own_chunk, jnp.broadcast_to(weights[j][r:r + 1, :], (ROW_CHUNKS, LANES)), 0.0)
                 for j in range(ROW_CHUNKS)], axis=1).astype(BF16)
            o_ref[n] = _dot(lhs, rows)

    _token_loop(o_ref.shape[0], group)


def _idx_spec(tb):
    return pl.BlockSpec((1, tb, PEER_PICKS), lambda i: (i, 0, 0), memory_space=pltpu.SMEM)


def _table_spec(tab):
    return pl.BlockSpec(tab.shape, lambda i: (0, 0), pipeline_mode=pl.Buffered(1))


def _peer_u(idx, xw, tab):
    n = idx.shape[0]
    tb = min(PEER_TOKENS, n)
    return pl.pallas_call(
        _peer_u_kernel,
        grid=(n // tb,),
        in_specs=[_idx_spec(tb), pl.BlockSpec((tb * ROW_WORDS, LANES), lambda i: (i, 0)),
                  _table_spec(tab)],
        out_specs=pl.BlockSpec((tb, LANES), lambda i: (i, 0)),
        out_shape=jax.ShapeDtypeStruct((n, LANES), F32),
        compiler_params=_params(("arbitrary",), VMEM_LIMIT_TABLE),
        name="peer_u",
    )(idx.reshape(n // tb, tb, PEER_PICKS), xw, tab)


def _peer_v(idx, wexp, tab):
    n = idx.shape[0]
    tb = min(PEER_TOKENS, n)
    out = pl.pallas_call(
        _peer_v_kernel,
        grid=(n // tb,),
        in_specs=[_idx_spec(tb),
                  pl.BlockSpec((ROW_CHUNKS, tb, LANES), lambda i: (0, i, 0)),
                  _table_spec(tab)],
        out_specs=pl.BlockSpec((tb, ROW_CHUNKS, LANES), lambda i: (i, 0, 0)),
        out_shape=jax.ShapeDtypeStruct((n, ROW_CHUNKS, LANES), F32),
        compiler_params=_params(("arbitrary",), VMEM_LIMIT_TABLE),
        name="peer_v",
    )(idx.reshape(n // tb, tb, PEER_PICKS), wexp, tab)
    return out.reshape(n * ROW_CHUNKS, LANES)


def _gate_act_kernel(a_ref, g_ref, e_ref, o_ref):
    a = a_ref[...]
    w = g_ref[...] * (0.5 * a * (1.0 + lax.erf(a * (2.0 ** -0.5))))
    wx = _dot(w.astype(BF16), e_ref[...])
    for j in range(ROW_CHUNKS):
        o_ref[j] = wx[:, j * LANES:(j + 1) * LANES]


def _gate_act(act, g):
    n = act.shape[0]
    tm = min(1024, n)
    spec = pl.BlockSpec((tm, LANES), lambda i: (i, 0))
    expand = (jnp.arange(D_MODEL)[None, :] // ROW_CHUNKS == jnp.arange(LANES)[:, None]).astype(BF16)
    return pl.pallas_call(
        _gate_act_kernel,
        grid=(n // tm,),
        in_specs=[spec, spec, pl.BlockSpec((LANES, D_MODEL), lambda i: (0, 0))],
        out_specs=pl.BlockSpec((ROW_CHUNKS, tm, LANES), lambda i: (0, i, 0)),
        out_shape=jax.ShapeDtypeStruct((ROW_CHUNKS, n, LANES), F32),
        compiler_params=_params(("parallel",)),
        name="peer_gate_act",
    )(act, g, expand)


def _peer(hb, xw, w_q, subkeys, u_tab, v_tab):
    qp = _matmul(hb, w_q.astype(BF16))
    idx_t, g_t = _peer_topk(qp, subkeys.astype(BF16))
    idx = idx_t.T
    act = _peer_u(idx, xw, _pack_table(u_tab))
    return _peer_v(idx, _gate_act(act, g_t.T), _pack_table(v_tab))


def kernel(x, mem, a_w_in, a_w_gate2, a_b_gate, a_head_g, a_w_mem_kv, a_w_out, b_w_in, b_w_mem_kv,
           b_w_out, sb_w_kv, peer_w_q, peer_subkeys, peer_u, peer_v, ln_g, ln_b):
    batch, seq, d = x.shape
    n = batch * seq
    h = x.reshape(n, d)
    memf = mem.reshape(batch * N_MEM, d)
    pad_w = 2 * GLA_DV
    s = [GLA_K, 2 * GLA_K, 2 * GLA_K + GLA_V, 2 * GLA_K + 2 * GLA_V,
         2 * GLA_K + 2 * GLA_V + GLA_GATE_RANK]

    w_in = a_w_in[0]
    w_in_r = jnp.concatenate(
        [w_in[:, :s[3]], w_in[:, s[4]:], w_in[:, s[3]:s[4]],
         jnp.zeros((d, pad_w - GLA_GATE_RANK), F32)], axis=1).astype(BF16)
    proj = _matmul(h, w_in_r)
    wg = jnp.concatenate([a_w_gate2[0], jnp.zeros((pad_w - GLA_GATE_RANK, GLA_K), F32)],
                         axis=0).astype(BF16)
    o = _gla(proj, wg, a_b_gate[0].reshape(1, GLA_K), a_head_g[0].reshape(1, GLA_V), batch, seq)
    kv_mem = _matmul(memf, a_w_mem_kv[0].astype(BF16))
    m = _mem_attn(proj, (2 * GLA_K + 2 * GLA_V) // MEM_W, kv_mem, batch, seq)
    w_out = a_w_out[0].astype(BF16)
    h, hb, xw = _proj_ln(o, m, h, w_out[:GLA_V], w_out[GLA_V:], ln_g[0, 0].reshape(1, d),
                         ln_b[0, 0].reshape(1, d))
    ffn = _peer(hb, xw, peer_w_q[0], peer_subkeys[0], peer_u[0], peer_v[0])
    h, hb = _res_ln(ffn, h, ln_g[0, 1].reshape(1, d), ln_b[0, 1].reshape(1, d))

    kv_sb = _matmul(hb, sb_w_kv.astype(BF16), out_dtype=BF16)
    proj = _matmul(hb, b_w_in[0].astype(BF16))
    o = _stick_breaking(proj, kv_sb, batch, seq)
    kv_mem = _matmul(memf, b_w_mem_kv[0].astype(BF16))
    m = _mem_attn(proj, SB_W // MEM_W, kv_mem, batch, seq)
    w_out = b_w_out[0].astype(BF16)
    h, hb, xw = _proj_ln(o, m, h, w_out[:SB_W], w_out[SB_W:], ln_g[1, 0].reshape(1, d),
                         ln_b[1, 0].reshape(1, d))
    ffn = _peer(hb, xw, peer_w_q[1], peer_subkeys[1], peer_u[1], peer_v[1])
    h, _ = _res_ln(ffn, h, ln_g[1, 1].reshape(1, d), ln_b[1, 1].reshape(1, d))
    return h.reshape(batch, seq, d)
```

```python
import functools

import jax
import jax.numpy as jnp
from jax import lax
from jax.experimental import pallas as pl
from jax.experimental.pallas import tpu as pltpu

F32 = jnp.float32
BF16 = jnp.bfloat16
I32 = jnp.int32

LANES = 128
SUBLANES = 8
VMEM_LIMIT_DEFAULT = 48 * 1024 * 1024
VMEM_LIMIT_TABLE = 56 * 1024 * 1024

D_MODEL = 1024
N_MEM = 256
GLA_HEADS = 6
GLA_DK = 64
GLA_DV = 128
GLA_K = GLA_HEADS * GLA_DK
GLA_V = GLA_HEADS * GLA_DV
GLA_GATE_RANK = 16
GLA_TAU = 16.0
GLA_CHUNK = 64
SB_HEADS = 12
SB_DIM = 64
SB_W = SB_HEADS * SB_DIM
SB_BLOCK = 128
MEM_HEADS = 4
MEM_DIM = 64
MEM_W = MEM_HEADS * MEM_DIM
PEER_HEADS = 8
PEER_KEYS = 128
PEER_EXPERTS = PEER_KEYS * PEER_KEYS
PEER_TOPK = 16
PEER_QHALF = 128
PEER_PICKS = PEER_HEADS * PEER_TOPK
DEPTH = 2
DEEPNORM_ALPHA = (2.0 * DEPTH) ** 0.25
EPS = 1e-5

ROW_CHUNKS = D_MODEL // LANES
ROW_WORDS = ROW_CHUNKS // 2

NT_DIMS = (((1,), (1,)), ((), ()))
TN_DIMS = (((0,), (0,)), ((), ()))


def _params(semantics, vmem=VMEM_LIMIT_DEFAULT):
    return pltpu.CompilerParams(dimension_semantics=semantics, vmem_limit_bytes=vmem)


def _log_sigmoid(z):
    return jnp.minimum(z, 0.0) - jnp.log1p(jnp.exp(-jnp.abs(z)))


def _dot(a, b):
    return jnp.dot(a, b, preferred_element_type=F32)


def _split_bf16(x, parts):
    out = []
    r = x
    for _ in range(parts):
        p = r.astype(BF16)
        out.append(p)
        r = r - p.astype(F32)
    return out


def _layer_norm(y, g, b):
    mu = jnp.mean(y, axis=-1, keepdims=True)
    yc = y - mu
    var = jnp.mean(yc * yc, axis=-1, keepdims=True)
    return yc * lax.rsqrt(var + EPS) * g + b


def _matmul_kernel(x_ref, w_ref, o_ref):
    o_ref[...] = _dot(x_ref[...].astype(BF16), w_ref[...]).astype(o_ref.dtype)


MATMUL_ROWS = 512


def _matmul(x, w, out_dtype=F32):
    m, k = x.shape
    n = w.shape[1]
    tm = min(MATMUL_ROWS, m)
    return pl.pallas_call(
        _matmul_kernel,
        grid=(m // tm,),
        in_specs=[pl.BlockSpec((tm, k), lambda i: (i, 0)),
                  pl.BlockSpec((k, n), lambda i: (0, 0))],
        out_specs=pl.BlockSpec((tm, n), lambda i: (i, 0)),
        out_shape=jax.ShapeDtypeStruct((m, n), out_dtype),
        compiler_params=_params(("parallel",)),
        name="matmul",
    )(x, w)


GLA_ROWS = 256


def _gla_kernel(q_ref, k_ref, v_ref, r_ref, glr_ref, wg_ref, bg_ref, hg_ref, o_ref, st_ref):
    @pl.when(pl.program_id(2) == 0)
    def _():
        st_ref[...] = jnp.zeros_like(st_ref)

    c_rows = GLA_CHUNK
    lane = lax.broadcasted_iota(I32, (c_rows, LANES), 1)
    head_masks = (lane < GLA_DK, lane >= GLA_DK)
    row = lax.broadcasted_iota(I32, (c_rows, c_rows), 0)
    col = lax.broadcasted_iota(I32, (c_rows, c_rows), 1)
    causal = row >= col
    tril = causal.astype(BF16)
    wg = wg_ref[...]
    bg = bg_ref[...]
    hg = hg_ref[...]
    chunks = range(q_ref.shape[0] // c_rows)
    heads = range(2)
    rows = [slice(c * c_rows, (c + 1) * c_rows) for c in chunks]
    cols = [slice(a * GLA_DV, (a + 1) * GLA_DV) for a in heads]
    z = [_dot(glr_ref[rows[c], :].astype(BF16), wg) + bg for c in chunks]
    lg_parts = [_split_bf16(_log_sigmoid(z[c]) * (1.0 / GLA_TAU), 3) for c in chunks]
    bc = [sum(_dot(tril, part) for part in lg_parts[c]) for c in chunks]
    bl = [bc[c][c_rows - 1:c_rows, :] for c in chunks]
    qd = [q_ref[rows[c], :] * (GLA_DK ** -0.5) * jnp.exp(bc[c]) for c in chunks]
    ki = [(k_ref[rows[c], :] * jnp.exp(-bc[c])).astype(BF16) for c in chunks]
    ke = [k_ref[rows[c], :] * jnp.exp(bl[c] - bc[c]) for c in chunks]
    dec = [jnp.exp(bl[c]) for c in chunks]
    qa = [[jnp.where(head_masks[a], qd[c], 0.0).astype(BF16) for a in heads] for c in chunks]
    kea = [[jnp.where(head_masks[a], ke[c], 0.0).astype(BF16) for a in heads] for c in chunks]
    va = [[v_ref[rows[c], cols[a]].astype(BF16) for a in heads] for c in chunks]
    attn = [[jnp.where(causal, lax.dot_general(qa[c][a], ki[c], NT_DIMS, preferred_element_type=F32),
                       0.0).astype(BF16) for a in heads] for c in chunks]
    inc = [[lax.dot_general(va[c][a], kea[c][a], TN_DIMS, preferred_element_type=F32)
            for a in heads] for c in chunks]
    o_intra = [[_dot(attn[c][a], va[c][a]) for a in heads] for c in chunks]
    states = [st_ref[0], st_ref[1]]
    for c in chunks:
        for a in heads:
            o = o_intra[c][a] + lax.dot_general(qa[c][a], states[a].astype(BF16), NT_DIMS,
                                                preferred_element_type=F32)
            states[a] = states[a] * dec[c] + inc[c][a]
            ms = jnp.mean(o * o, axis=-1, keepdims=True)
            r = r_ref[rows[c], cols[a]]
            silu = r / (1.0 + jnp.exp(-r))
            o_ref[rows[c], cols[a]] = o * lax.rsqrt(ms + EPS) * hg[:, cols[a]] * silu
    st_ref[0] = states[0]
    st_ref[1] = states[1]


def _gla(proj, wg, bg, hg, batch, seq):
    n = proj.shape[0]
    tr = min(GLA_ROWS, seq)
    nt = seq // tr
    rowmap = lambda b, j, t: b * nt + t
    pair_w = 2 * GLA_DV
    return pl.pallas_call(
        _gla_kernel,
        grid=(batch, GLA_HEADS // 2, nt),
        in_specs=[
            pl.BlockSpec((tr, LANES), lambda b, j, t: (rowmap(b, j, t), j)),
            pl.BlockSpec((tr, LANES), lambda b, j, t: (rowmap(b, j, t), GLA_K // LANES + j)),
            pl.BlockSpec((tr, pair_w), lambda b, j, t: (rowmap(b, j, t), 2 * GLA_K // pair_w + j)),
            pl.BlockSpec((tr, pair_w),
                         lambda b, j, t: (rowmap(b, j, t), (2 * GLA_K + GLA_V) // pair_w + j)),
            pl.BlockSpec((tr, pair_w),
                         lambda b, j, t: (rowmap(b, j, t), (2 * GLA_K + 2 * GLA_V + MEM_W) // pair_w)),
            pl.BlockSpec((pair_w, LANES), lambda b, j, t: (0, j)),
            pl.BlockSpec((1, LANES), lambda b, j, t: (0, j)),
            pl.BlockSpec((1, pair_w), lambda b, j, t: (0, j)),
        ],
        out_specs=pl.BlockSpec((tr, pair_w), lambda b, j, t: (rowmap(b, j, t), j)),
        out_shape=jax.ShapeDtypeStruct((n, GLA_V), F32),
        scratch_shapes=[pltpu.VMEM((2, GLA_DV, LANES), F32)],
        compiler_params=_params(("parallel", "parallel", "arbitrary")),
        name="gla",
    )(proj, proj, proj, proj, proj, wg, bg, hg)


MEM_ROWS = 512


def _mem_attn_kernel(q_ref, kv_ref, o_ref):
    tq = q_ref.shape[0]
    lane = lax.broadcasted_iota(I32, (tq, LANES), 1)
    lo_head = lane < MEM_DIM
    for j in range(MEM_W // LANES):
        cols = slice(j * LANES, (j + 1) * LANES)
        q2 = q_ref[:, cols]
        k2 = kv_ref[:, cols].astype(BF16)
        v2 = kv_ref[:, MEM_W + j * LANES:MEM_W + (j + 1) * LANES].astype(BF16)
        outs = []
        for a in range(2):
            qa = jnp.where(lo_head if a == 0 else ~lo_head, q2, 0.0).astype(BF16)
            s = lax.dot_general(qa, k2, NT_DIMS, preferred_element_type=F32) * (MEM_DIM ** -0.5)
            e = jnp.exp(s - jnp.max(s, axis=-1, keepdims=True))
            p = e / jnp.sum(e, axis=-1, keepdims=True)
            outs.append(_dot(p.astype(BF16), v2))
        o_ref[:, cols] = jnp.where(lo_head, outs[0], outs[1])


def _mem_attn(proj, q_col_block, kv, batch, seq):
    n = proj.shape[0]
    tq = min(MEM_ROWS, seq)
    nt = seq // tq
    return pl.pallas_call(
        _mem_attn_kernel,
        grid=(batch, nt),
        in_specs=[pl.BlockSpec((tq, MEM_W), lambda b, t: (b * nt + t, q_col_block)),
                  pl.BlockSpec((N_MEM, 2 * MEM_W), lambda b, t: (b, 0))],
        out_specs=pl.BlockSpec((tq, MEM_W), lambda b, t: (b * nt + t, 0)),
        out_shape=jax.ShapeDtypeStruct((n, MEM_W), F32),
        compiler_params=_params(("parallel", "parallel")),
        name="mem_attn",
    )(proj, kv)


SB_TILE = 256


SB_MASKED = -1e30


def _sb_kernel(q_ref, k_ref, v_ref, o_ref, *scratch):
    acc_refs, run_refs = scratch[0:2], scratch[2:4]
    stage = (scratch[4:10], scratch[10:16])
    i = pl.program_id(2)
    t = q_ref.shape[0]
    heads = range(2)
    row = lax.broadcasted_iota(I32, (t, t), 0)
    col = lax.broadcasted_iota(I32, (t, t), 1)
    later = (row > col).astype(BF16)
    lo_head = lax.broadcasted_iota(I32, (t, LANES), 1) < SB_DIM
    q2 = q_ref[...] * (SB_DIM ** -0.5)
    qa = (jnp.where(lo_head, q2, 0.0).astype(BF16), jnp.where(lo_head, 0.0, q2).astype(BF16))
    for ref in acc_refs + run_refs:
        ref[...] = jnp.zeros_like(ref)

    def scores(j, p, diag):
        kj = k_ref[pl.ds(pl.multiple_of(j * t, t), t), :]
        z = [lax.dot_general(qa[a], kj, NT_DIMS, preferred_element_type=F32) for a in heads]
        for a in heads:
            ls = jnp.minimum(z[a], 0.0) - jnp.log(1.0 + jnp.exp(-jnp.abs(z[a])))
            ln = ls - z[a]
            if diag:
                visible = col < row
                ln = jnp.where(visible, ln, 0.0)
                ls = jnp.where(visible, ls, SB_MASKED)
            hi, lo = _split_bf16(ln, 2)
            stage[p][a][...] = ls
            stage[p][2 + a][...] = hi
            stage[p][4 + a][...] = lo

    def apply(j, p):
        vj = v_ref[pl.ds(pl.multiple_of(j * t, t), t), :]
        hi = [stage[p][2 + a][...] for a in heads]
        lo = [stage[p][4 + a][...] for a in heads]
        exc = [_dot(hi[a], later) + _dot(lo[a], later) for a in heads]
        run = [run_refs[a][...] for a in heads]
        a_w = [jnp.exp(stage[p][a][...] + exc[a] + jnp.concatenate([run[a]] * (t // LANES), axis=1))
               for a in heads]
        out = [_dot(a_w[a].astype(BF16), vj) for a in heads]
        for a in heads:
            acc_refs[a][...] += out[a]
            total = exc[a][:, 0:1] + hi[a][:, 0:1].astype(F32) + lo[a][:, 0:1].astype(F32)
            run_refs[a][...] = run[a] + jnp.broadcast_to(total, (t, LANES))

    scores(i, 0, True)

    def two_steps(m, carry):
        j = i - 2 * m
        scores(j - 1, 1, False)
        apply(j, 0)
        scores(j - 2, 0, False)
        apply(j - 1, 1)
        return carry

    lax.fori_loop(0, i // 2, two_steps, 0)

    @pl.when(i % 2 == 1)
    def _():
        scores(0, 1, False)
        apply(1, 0)
        apply(0, 1)

    @pl.when(i % 2 == 0)
    def _():
        apply(0, 0)

    o_ref[...] = jnp.where(lo_head, acc_refs[0][...], acc_refs[1][...])


def _stick_breaking(qproj, kv, batch, seq):
    n = qproj.shape[0]
    t = min(SB_TILE, seq)
    nq = seq // t
    pairs = SB_W // LANES
    return pl.pallas_call(
        _sb_kernel,
        grid=(batch, pairs, nq),
        in_specs=[pl.BlockSpec((t, LANES), lambda b, j, i: (b * nq + i, j)),
                  pl.BlockSpec((seq, LANES), lambda b, j, i: (b, j)),
                  pl.BlockSpec((seq, LANES), lambda b, j, i: (b, pairs + j))],
        out_specs=pl.BlockSpec((t, LANES), lambda b, j, i: (b * nq + i, j)),
        out_shape=jax.ShapeDtypeStruct((n, SB_W), F32),
        scratch_shapes=([pltpu.VMEM((t, LANES), F32)] * 4
                        + ([pltpu.VMEM((t, t), F32)] * 2 + [pltpu.VMEM((t, t), BF16)] * 4) * 2),
        compiler_params=_params(("parallel", "parallel", "arbitrary")),
        name="stick_breaking",
    )(qproj, kv, kv)


LN_ROWS = 256


def _proj_ln_kernel(a1_ref, a2_ref, h_ref, w1_ref, w2_ref, g_ref, b_ref, o_ref, ob_ref, xw_ref):
    mix = _dot(a1_ref[...].astype(BF16), w1_ref[...]) + _dot(a2_ref[...].astype(BF16), w2_ref[...])
    y = _layer_norm(DEEPNORM_ALPHA * h_ref[...] + mix, g_ref[...], b_ref[...])
    o_ref[...] = y
    ob_ref[...] = y.astype(BF16)
    _store_packed_rows(y, xw_ref)


def _ln_out(h, tm):
    spec = pl.BlockSpec((tm, h.shape[1]), lambda i: (i, 0))
    return [spec, spec], [jax.ShapeDtypeStruct(h.shape, F32), jax.ShapeDtypeStruct(h.shape, BF16)]


def _proj_ln(a1, a2, h, w1, w2, g, b):
    n = h.shape[0]
    tm = min(LN_ROWS, n)
    full = lambda arr: pl.BlockSpec(arr.shape, lambda i: (0, 0))
    rows = lambda arr: pl.BlockSpec((tm, arr.shape[1]), lambda i: (i, 0))
    out_specs, out_shape = _ln_out(h, tm)
    out_specs = out_specs + [pl.BlockSpec((tm * ROW_WORDS, LANES), lambda i: (i, 0))]
    out_shape = out_shape + [jax.ShapeDtypeStruct((n * ROW_WORDS, LANES), I32)]
    return pl.pallas_call(
        _proj_ln_kernel,
        grid=(n // tm,),
        in_specs=[rows(a1), rows(a2), rows(h), full(w1), full(w2), full(g), full(b)],
        out_specs=out_specs,
        out_shape=out_shape,
        compiler_params=_params(("parallel",)),
        name="proj_ln",
    )(a1, a2, h, w1, w2, g, b)


def _res_ln_kernel(f_ref, h_ref, g_ref, b_ref, o_ref, ob_ref):
    tm = h_ref.shape[0]
    f = jnp.concatenate([f_ref[pl.ds(c, tm, stride=ROW_CHUNKS), :] for c in range(ROW_CHUNKS)], axis=1)
    y = _layer_norm(DEEPNORM_ALPHA * h_ref[...] + f, g_ref[...], b_ref[...])
    o_ref[...] = y
    ob_ref[...] = y.astype(BF16)


def _res_ln(f, h, g, b):
    n = h.shape[0]
    tm = min(512, n)
    full = lambda arr: pl.BlockSpec(arr.shape, lambda i: (0, 0))
    rows = lambda arr: pl.BlockSpec((tm, arr.shape[1]), lambda i: (i, 0))
    out_specs, out_shape = _ln_out(h, tm)
    return pl.pallas_call(
        _res_ln_kernel,
        grid=(n // tm,),
        in_specs=[pl.BlockSpec((tm * ROW_CHUNKS, LANES), lambda i: (i, 0)), rows(h), full(g), full(b)],
        out_specs=out_specs,
        out_shape=out_shape,
        compiler_params=_params(("parallel",)),
        name="res_ln",
    )(f, h, g, b)


TOPK_TOKENS = 1024


def _top16_rows(problems):
    state = []
    for vals, payload in problems:
        k = vals.shape[0]
        rows = lax.broadcasted_iota(I32, (k // 2, vals.shape[1]), 0).astype(F32)
        first, second = vals[:k // 2], vals[k // 2:]
        keep = first >= second
        slot = {"hi": jnp.where(keep, first, second), "lo": jnp.where(keep, second, first),
                "row_hi": jnp.where(keep, rows, rows + float(k // 2)),
                "row_lo": jnp.where(keep, rows + float(k // 2), rows), "k": float(k)}
        if payload is not None:
            slot["pay_hi"] = jnp.where(keep, payload[:k // 2], payload[k // 2:])
            slot["pay_lo"] = jnp.where(keep, payload[k // 2:], payload[:k // 2])
        state.append(slot)
    outs = [([], []) for _ in problems]
    for _ in range(PEER_TOPK):
        for p, slot in enumerate(state):
            m = jnp.max(slot["hi"], axis=0, keepdims=True)
            tied_rows = jnp.where(slot["hi"] == m, slot["row_hi"], slot["k"])
            am = jnp.min(tied_rows, axis=0, keepdims=True)
            hit = tied_rows == am
            outs[p][0].append(m)
            if "pay_hi" in slot:
                outs[p][1].append(jnp.max(jnp.where(hit, slot["pay_hi"], -1.0), axis=0, keepdims=True))
                slot["pay_hi"] = jnp.where(hit, slot["pay_lo"], slot["pay_hi"])
            else:
                outs[p][1].append(am)
            slot["hi"] = jnp.where(hit, slot["lo"], slot["hi"])
            slot["row_hi"] = jnp.where(hit, slot["row_lo"], slot["row_hi"])
            slot["lo"] = jnp.where(hit, -jnp.inf, slot["lo"])
    return outs


def _pair_candidates(top0, top1):
    (s0, i0), (s1, i1) = top0, top1
    s0_all = jnp.concatenate(s0, axis=0)
    i0_all = jnp.concatenate(i0, axis=0)
    s1_all = jnp.concatenate(s1, axis=0)
    i1_all = jnp.concatenate(i1, axis=0)
    half = PEER_TOPK // 2
    sub = lax.broadcasted_iota(I32, (half, s0_all.shape[1]), 0)
    cand_s = [s0[0] + s1_all]
    cand_i = [i0[0] * float(PEER_KEYS) + i1_all]
    for i in range(1, half):
        cand_s.append(jnp.where(sub < PEER_TOPK // (i + 1), s0[i] + s1_all[:half], -jnp.inf))
        cand_i.append(i0[i] * float(PEER_KEYS) + i1_all[:half])
    cand_s.append(s0_all[half:] + s1[0])
    cand_i.append(i0_all[half:] * float(PEER_KEYS) + i1[0])
    return jnp.concatenate(cand_s, axis=0), jnp.concatenate(cand_i, axis=0)


def _peer_topk_kernel(q_ref, sk_ref, idx_ref, g_ref):
    groups = q_ref.shape[0] // LANES
    scores = []
    for grp in range(groups):
        for p in range(2):
            qp = q_ref[grp * LANES:(grp + 1) * LANES, p * PEER_QHALF:(p + 1) * PEER_QHALF]
            s_t = lax.dot_general(sk_ref[0, p], qp.astype(BF16), NT_DIMS, preferred_element_type=F32)
            scores.append((s_t, None))
    tops = _top16_rows(scores)
    best = _top16_rows([_pair_candidates(tops[2 * grp], tops[2 * grp + 1]) for grp in range(groups)])
    for grp, (best_s, best_i) in enumerate(best):
        cols = slice(grp * LANES, (grp + 1) * LANES)
        e = [jnp.exp(s - best_s[0]) for s in best_s]
        g_ref[:, cols] = jnp.concatenate(e, axis=0) / sum(e)
        idx_ref[:, cols] = (jnp.concatenate(best_i, axis=0) * float(ROW_WORDS)).astype(I32)


def _peer_topk(qp, subkeys):
    n = qp.shape[0]
    tt = min(TOPK_TOKENS, n)
    spec_out = pl.BlockSpec((PEER_TOPK, tt), lambda i, h: (h, i))
    return pl.pallas_call(
        _peer_topk_kernel,
        grid=(n // tt, PEER_HEADS),
        in_specs=[pl.BlockSpec((tt, 2 * PEER_QHALF), lambda i, h: (i, h)),
                  pl.BlockSpec((1, 2, PEER_KEYS, PEER_QHALF), lambda i, h: (h, 0, 0, 0))],
        out_specs=[spec_out, spec_out],
        out_shape=[jax.ShapeDtypeStruct((PEER_PICKS, n), I32),
                   jax.ShapeDtypeStruct((PEER_PICKS, n), F32)],
        compiler_params=_params(("parallel", "parallel")),
        name="peer_topk",
    )(qp, subkeys)


PACK_ROWS = 512


def _store_packed_rows(x, o_ref):
    r = x.shape[0]
    bits = lax.bitcast_convert_type(x.astype(BF16).astype(F32), I32)
    for s in range(ROW_WORDS):
        lo = lax.shift_right_logical(bits[:, (2 * s) * LANES:(2 * s + 1) * LANES], 16)
        hi = bits[:, (2 * s + 1) * LANES:(2 * s + 2) * LANES]
        o_ref[pl.ds(s, r, stride=ROW_WORDS), :] = hi | lo


def _pack_table_kernel(x_ref, o_ref):
    _store_packed_rows(x_ref[0], o_ref)


def _pack_table(tabs, layer):
    e = tabs.shape[1]
    tr = min(PACK_ROWS, e)
    return pl.pallas_call(
        _pack_table_kernel,
        grid=(e // tr,),
        in_specs=[pl.BlockSpec((1, tr, D_MODEL), lambda i: (layer, i, 0))],
        out_specs=pl.BlockSpec((tr * ROW_WORDS, LANES), lambda i: (i, 0)),
        out_shape=jax.ShapeDtypeStruct((e * ROW_WORDS, LANES), I32),
        compiler_params=_params(("parallel",)),
        name="pack_table",
    )(tabs)


PEER_TOKENS = 128
PEER_UNROLL = 32
PAIRS = PEER_PICKS // 2


def _gather_rows(idx_ref, tab_ref, n):
    pieces = []
    for k in range(PAIRS):
        e0 = idx_ref[0, n, 2 * k]
        e1 = idx_ref[0, n, 2 * k + 1]
        words = jnp.concatenate(
            [tab_ref[pl.ds(pl.multiple_of(e, ROW_WORDS), ROW_WORDS), :] for e in (e0, e1)],
            axis=0)
        pieces.append(pltpu.bitcast(words, BF16))
    return pieces


def _token_loop(n_tokens, group):
    def body(m, carry):
        for q in range(PEER_UNROLL // SUBLANES):
            group(pl.multiple_of(m * PEER_UNROLL + q * SUBLANES, SUBLANES))
        return carry

    lax.fori_loop(0, n_tokens // PEER_UNROLL, body, 0)


def _peer_u_kernel(idx_ref, x_ref, tab_ref, o_ref):
    ones = jnp.ones((LANES, LANES), BF16)
    lane = lax.broadcasted_iota(I32, (ROW_CHUNKS, LANES), 1)

    def token(n):
        x4 = x_ref[pl.ds(pl.multiple_of(n * ROW_WORDS, ROW_WORDS), ROW_WORDS), :]
        xw = pltpu.bitcast(jnp.concatenate([x4, x4], axis=0), BF16)
        prods = [piece * xw for piece in _gather_rows(idx_ref, tab_ref, n)]
        z = _dot(jnp.concatenate(prods, axis=0), ones)
        n_acc = 4
        acc = [jnp.zeros((ROW_CHUNKS, LANES), F32) for _ in range(n_acc)]
        for e in range(PEER_PICKS):
            acc[e % n_acc] = acc[e % n_acc] + jnp.where(
                lane == e, z[e * ROW_CHUNKS:(e + 1) * ROW_CHUNKS, :], 0.0)
        return jnp.sum(sum(acc), axis=0, keepdims=True)

    def group(base):
        o_ref[pl.ds(base, SUBLANES), :] = jnp.concatenate(
            [token(base + r) for r in range(SUBLANES)], axis=0)

    _token_loop(o_ref.shape[0], group)


def _peer_v_kernel(idx_ref, wexp_ref, tab_ref, o_ref):
    lane = lax.broadcasted_iota(I32, (ROW_CHUNKS, LANES), 1)
    sub = lax.broadcasted_iota(I32, (ROW_CHUNKS, LANES), 0)
    own_chunk = (lane & (ROW_CHUNKS - 1)) == sub

    def group(base):
        weights = [wexp_ref[j, pl.ds(base, SUBLANES), :] for j in range(ROW_CHUNKS)]
        for r in range(SUBLANES):
            n = base + r
            rows = jnp.concatenate(_gather_rows(idx_ref, tab_ref, n), axis=0)
            lhs = jnp.concatenate(
                [jnp.where(own_chunk, jnp.broadcast_to(weights[j][r:r + 1, :], (ROW_CHUNKS, LANES)), 0.0)
                 for j in range(ROW_CHUNKS)], axis=1).astype(BF16)
            o_ref[n] = _dot(lhs, rows)

    _token_loop(o_ref.shape[0], group)


def _idx_spec(tb):
    return pl.BlockSpec((1, tb, PEER_PICKS), lambda i: (i, 0, 0), memory_space=pltpu.SMEM)


def _table_spec(tab):
    return pl.BlockSpec(tab.shape, lambda i: (0, 0), pipeline_mode=pl.Buffered(1))


def _peer_u(idx, xw, tab):
    n = idx.shape[0]
    tb = min(PEER_TOKENS, n)
    return pl.pallas_call(
        _peer_u_kernel,
        grid=(n // tb,),
        in_specs=[_idx_spec(tb), pl.BlockSpec((tb * ROW_WORDS, LANES), lambda i: (i, 0)),
                  _table_spec(tab)],
        out_specs=pl.BlockSpec((tb, LANES), lambda i: (i, 0)),
        out_shape=jax.ShapeDtypeStruct((n, LANES), F32),
        compiler_params=_params(("arbitrary",), VMEM_LIMIT_TABLE),
        name="peer_u",
    )(idx.reshape(n // tb, tb, PEER_PICKS), xw, tab)


def _peer_v(idx, wexp, tab):
    n = idx.shape[0]
    tb = min(PEER_TOKENS, n)
    out = pl.pallas_call(
        _peer_v_kernel,
        grid=(n // tb,),
        in_specs=[_idx_spec(tb),
                  pl.BlockSpec((ROW_CHUNKS, tb, LANES), lambda i: (0, i, 0)),
                  _table_spec(tab)],
        out_specs=pl.BlockSpec((tb, ROW_CHUNKS, LANES), lambda i: (i, 0, 0)),
        out_shape=jax.ShapeDtypeStruct((n, ROW_CHUNKS, LANES), F32),
        compiler_params=_params(("arbitrary",), VMEM_LIMIT_TABLE),
        name="peer_v",
    )(idx.reshape(n // tb, tb, PEER_PICKS), wexp, tab)
    return out.reshape(n * ROW_CHUNKS, LANES)


def _gate_act_kernel(a_ref, g_ref, e_ref, o_ref):
    a = a_ref[...]
    w = g_ref[...] * (0.5 * a * (1.0 + lax.erf(a * (2.0 ** -0.5))))
    wx = _dot(w.astype(BF16), e_ref[...])
    for j in range(ROW_CHUNKS):
        o_ref[j] = wx[:, j * LANES:(j + 1) * LANES]


def _gate_act(act, g):
    n = act.shape[0]
    tm = min(1024, n)
    spec = pl.BlockSpec((tm, LANES), lambda i: (i, 0))
    expand = (jnp.arange(D_MODEL)[None, :] // ROW_CHUNKS == jnp.arange(LANES)[:, None]).astype(BF16)
    return pl.pallas_call(
        _gate_act_kernel,
        grid=(n // tm,),
        in_specs=[spec, spec, pl.BlockSpec((LANES, D_MODEL), lambda i: (0, 0))],
        out_specs=pl.BlockSpec((ROW_CHUNKS, tm, LANES), lambda i: (0, i, 0)),
        out_shape=jax.ShapeDtypeStruct((ROW_CHUNKS, n, LANES), F32),
        compiler_params=_params(("parallel",)),
        name="peer_gate_act",
    )(act, g, expand)


def _peer(hb, xw, w_q, subkeys, u_tabs, v_tabs, layer):
    qp = _matmul(hb, w_q.astype(BF16))
    idx_t, g_t = _peer_topk(qp, subkeys.astype(BF16))
    idx = idx_t.T
    act = _peer_u(idx, xw, _pack_table(u_tabs, layer))
    return _peer_v(idx, _gate_act(act, g_t.T), _pack_table(v_tabs, layer))


def kernel(x, mem, a_w_in, a_w_gate2, a_b_gate, a_head_g, a_w_mem_kv, a_w_out, b_w_in, b_w_mem_kv,
           b_w_out, sb_w_kv, peer_w_q, peer_subkeys, peer_u, peer_v, ln_g, ln_b):
    batch, seq, d = x.shape
    n = batch * seq
    h = x.reshape(n, d)
    memf = mem.reshape(batch * N_MEM, d)
    pad_w = 2 * GLA_DV
    s = [GLA_K, 2 * GLA_K, 2 * GLA_K + GLA_V, 2 * GLA_K + 2 * GLA_V,
         2 * GLA_K + 2 * GLA_V + GLA_GATE_RANK]

    w_in = a_w_in[0]
    w_in_r = jnp.concatenate(
        [w_in[:, :s[3]], w_in[:, s[4]:], w_in[:, s[3]:s[4]],
         jnp.zeros((d, pad_w - GLA_GATE_RANK), F32)], axis=1).astype(BF16)
    proj = _matmul(h, w_in_r)
    wg = jnp.concatenate([a_w_gate2[0], jnp.zeros((pad_w - GLA_GATE_RANK, GLA_K), F32)],
                         axis=0).astype(BF16)
    o = _gla(proj, wg, a_b_gate[0].reshape(1, GLA_K), a_head_g[0].reshape(1, GLA_V), batch, seq)
    kv_mem = _matmul(memf, a_w_mem_kv[0].astype(BF16))
    m = _mem_attn(proj, (2 * GLA_K + 2 * GLA_V) // MEM_W, kv_mem, batch, seq)
    w_out = a_w_out[0].astype(BF16)
    h, hb, xw = _proj_ln(o, m, h, w_out[:GLA_V], w_out[GLA_V:], ln_g[0, 0].reshape(1, d),
                         ln_b[0, 0].reshape(1, d))
    ffn = _peer(hb, xw, peer_w_q[0], peer_subkeys[0], peer_u, peer_v, 0)
    h, hb = _res_ln(ffn, h, ln_g[0, 1].reshape(1, d), ln_b[0, 1].reshape(1, d))

    kv_sb = _matmul(hb, sb_w_kv.astype(BF16), out_dtype=BF16)
    proj = _matmul(hb, b_w_in[0].astype(BF16))
    o = _stick_breaking(proj, kv_sb, batch, seq)
    kv_mem = _matmul(memf, b_w_mem_kv[0].astype(BF16))
    m = _mem_attn(proj, SB_W // MEM_W, kv_mem, batch, seq)
    w_out = b_w_out[0].astype(BF16)
    h, hb, xw = _proj_ln(o, m, h, w_out[:SB_W], w_out[SB_W:], ln_g[1, 0].reshape(1, d),
                         ln_b[1, 0].reshape(1, d))
    ffn = _peer(hb, xw, peer_w_q[1], peer_subkeys[1], peer_u, peer_v, 1)
    h, _ = _res_ln(ffn, h, ln_g[1, 1].reshape(1, d), ln_b[1, 1].reshape(1, d))
    return h.reshape(batch, seq, d)
```

```python
import functools

import jax
import jax.numpy as jnp
from jax import lax
from jax.experimental import pallas as pl
from jax.experimental.pallas import tpu as pltpu

F32 = jnp.float32
BF16 = jnp.bfloat16
I32 = jnp.int32

LANES = 128
SUBLANES = 8
VMEM_LIMIT_DEFAULT = 48 * 1024 * 1024
VMEM_LIMIT_TABLE = 56 * 1024 * 1024

D_MODEL = 1024
N_MEM = 256
GLA_HEADS = 6
GLA_DK = 64
GLA_DV = 128
GLA_K = GLA_HEADS * GLA_DK
GLA_V = GLA_HEADS * GLA_DV
GLA_GATE_RANK = 16
GLA_TAU = 16.0
GLA_CHUNK = 64
SB_HEADS = 12
SB_DIM = 64
SB_W = SB_HEADS * SB_DIM
SB_BLOCK = 128
MEM_HEADS = 4
MEM_DIM = 64
MEM_W = MEM_HEADS * MEM_DIM
PEER_HEADS = 8
PEER_KEYS = 128
PEER_EXPERTS = PEER_KEYS * PEER_KEYS
PEER_TOPK = 16
PEER_QHALF = 128
PEER_PICKS = PEER_HEADS * PEER_TOPK
DEPTH = 2
DEEPNORM_ALPHA = (2.0 * DEPTH) ** 0.25
EPS = 1e-5

ROW_CHUNKS = D_MODEL // LANES
ROW_WORDS = ROW_CHUNKS // 2

NT_DIMS = (((1,), (1,)), ((), ()))
TN_DIMS = (((0,), (0,)), ((), ()))


def _params(semantics, vmem=VMEM_LIMIT_DEFAULT):
    return pltpu.CompilerParams(dimension_semantics=semantics, vmem_limit_bytes=vmem)


def _log_sigmoid(z):
    return jnp.minimum(z, 0.0) - jnp.log1p(jnp.exp(-jnp.abs(z)))


def _dot(a, b):
    return jnp.dot(a, b, preferred_element_type=F32)


def _split_bf16(x, parts):
    out = []
    r = x
    for _ in range(parts):
        p = r.astype(BF16)
        out.append(p)
        r = r - p.astype(F32)
    return out


def _layer_norm(y, g, b):
    mu = jnp.mean(y, axis=-1, keepdims=True)
    yc = y - mu
    var = jnp.mean(yc * yc, axis=-1, keepdims=True)
    return yc * lax.rsqrt(var + EPS) * g + b


def _matmul_kernel(x_ref, w_ref, o_ref):
    o_ref[...] = _dot(x_ref[...].astype(BF16), w_ref[...]).astype(o_ref.dtype)


MATMUL_ROWS = 512


def _matmul(x, w, out_dtype=F32):
    m, k = x.shape
    n = w.shape[1]
    tm = min(MATMUL_ROWS, m)
    return pl.pallas_call(
        _matmul_kernel,
        grid=(m // tm,),
        in_specs=[pl.BlockSpec((tm, k), lambda i: (i, 0)),
                  pl.BlockSpec((k, n), lambda i: (0, 0))],
        out_specs=pl.BlockSpec((tm, n), lambda i: (i, 0)),
        out_shape=jax.ShapeDtypeStruct((m, n), out_dtype),
        compiler_params=_params(("parallel",)),
        name="matmul",
    )(x, w)


GLA_ROWS = 256


def _gla_kernel(q_ref, k_ref, v_ref, r_ref, glr_ref, wg_ref, bg_ref, hg_ref, o_ref, st_ref):
    @pl.when(pl.program_id(2) == 0)
    def _():
        st_ref[...] = jnp.zeros_like(st_ref)

    c_rows = GLA_CHUNK
    lane = lax.broadcasted_iota(I32, (c_rows, LANES), 1)
    head_masks = (lane < GLA_DK, lane >= GLA_DK)
    row = lax.broadcasted_iota(I32, (c_rows, c_rows), 0)
    col = lax.broadcasted_iota(I32, (c_rows, c_rows), 1)
    causal = row >= col
    tril = causal.astype(BF16)
    wg = wg_ref[...]
    bg = bg_ref[...]
    hg = hg_ref[...]
    chunks = range(q_ref.shape[0] // c_rows)
    heads = range(2)
    rows = [slice(c * c_rows, (c + 1) * c_rows) for c in chunks]
    cols = [slice(a * GLA_DV, (a + 1) * GLA_DV) for a in heads]
    z = [_dot(glr_ref[rows[c], :].astype(BF16), wg) + bg for c in chunks]
    lg_parts = [_split_bf16(_log_sigmoid(z[c]) * (1.0 / GLA_TAU), 3) for c in chunks]
    bc = [sum(_dot(tril, part) for part in lg_parts[c]) for c in chunks]
    bl = [bc[c][c_rows - 1:c_rows, :] for c in chunks]
    qd = [q_ref[rows[c], :] * (GLA_DK ** -0.5) * jnp.exp(bc[c]) for c in chunks]
    ki = [(k_ref[rows[c], :] * jnp.exp(-bc[c])).astype(BF16) for c in chunks]
    ke = [k_ref[rows[c], :] * jnp.exp(bl[c] - bc[c]) for c in chunks]
    dec = [jnp.exp(bl[c]) for c in chunks]
    qa = [[jnp.where(head_masks[a], qd[c], 0.0).astype(BF16) for a in heads] for c in chunks]
    kea = [[jnp.where(head_masks[a], ke[c], 0.0).astype(BF16) for a in heads] for c in chunks]
    va = [[v_ref[rows[c], cols[a]].astype(BF16) for a in heads] for c in chunks]
    attn = [[jnp.where(causal, lax.dot_general(qa[c][a], ki[c], NT_DIMS, preferred_element_type=F32),
                       0.0).astype(BF16) for a in heads] for c in chunks]
    inc = [[lax.dot_general(va[c][a], kea[c][a], TN_DIMS, preferred_element_type=F32)
            for a in heads] for c in chunks]
    o_intra = [[_dot(attn[c][a], va[c][a]) for a in heads] for c in chunks]
    states = [st_ref[0], st_ref[1]]
    for c in chunks:
        for a in heads:
            o = o_intra[c][a] + lax.dot_general(qa[c][a], states[a].astype(BF16), NT_DIMS,
                                                preferred_element_type=F32)
            states[a] = states[a] * dec[c] + inc[c][a]
            ms = jnp.mean(o * o, axis=-1, keepdims=True)
            r = r_ref[rows[c], cols[a]]
            silu = r / (1.0 + jnp.exp(-r))
            o_ref[rows[c], cols[a]] = o * lax.rsqrt(ms + EPS) * hg[:, cols[a]] * silu
    st_ref[0] = states[0]
    st_ref[1] = states[1]


def _gla(proj, wg, bg, hg, batch, seq):
    n = proj.shape[0]
    tr = min(GLA_ROWS, seq)
    nt = seq // tr
    rowmap = lambda b, j, t: b * nt + t
    pair_w = 2 * GLA_DV
    return pl.pallas_call(
        _gla_kernel,
        grid=(batch, GLA_HEADS // 2, nt),
        in_specs=[
            pl.BlockSpec((tr, LANES), lambda b, j, t: (rowmap(b, j, t), j)),
            pl.BlockSpec((tr, LANES), lambda b, j, t: (rowmap(b, j, t), GLA_K // LANES + j)),
            pl.BlockSpec((tr, pair_w), lambda b, j, t: (rowmap(b, j, t), 2 * GLA_K // pair_w + j)),
            pl.BlockSpec((tr, pair_w),
                         lambda b, j, t: (rowmap(b, j, t), (2 * GLA_K + GLA_V) // pair_w + j)),
            pl.BlockSpec((tr, pair_w),
                         lambda b, j, t: (rowmap(b, j, t), (2 * GLA_K + 2 * GLA_V + MEM_W) // pair_w)),
            pl.BlockSpec((pair_w, LANES), lambda b, j, t: (0, j)),
            pl.BlockSpec((1, LANES), lambda b, j, t: (0, j)),
            pl.BlockSpec((1, pair_w), lambda b, j, t: (0, j)),
        ],
        out_specs=pl.BlockSpec((tr, pair_w), lambda b, j, t: (rowmap(b, j, t), j)),
        out_shape=jax.ShapeDtypeStruct((n, GLA_V), F32),
        scratch_shapes=[pltpu.VMEM((2, GLA_DV, LANES), F32)],
        compiler_params=_params(("parallel", "parallel", "arbitrary")),
        name="gla",
    )(proj, proj, proj, proj, proj, wg, bg, hg)


MEM_ROWS = 512


def _mem_attn_kernel(q_ref, kv_ref, o_ref):
    tq = q_ref.shape[0]
    lane = lax.broadcasted_iota(I32, (tq, LANES), 1)
    lo_head = lane < MEM_DIM
    for j in range(MEM_W // LANES):
        cols = slice(j * LANES, (j + 1) * LANES)
        q2 = q_ref[:, cols]
        k2 = kv_ref[:, cols].astype(BF16)
        v2 = kv_ref[:, MEM_W + j * LANES:MEM_W + (j + 1) * LANES].astype(BF16)
        outs = []
        for a in range(2):
            qa = jnp.where(lo_head if a == 0 else ~lo_head, q2, 0.0).astype(BF16)
            s = lax.dot_general(qa, k2, NT_DIMS, preferred_element_type=F32) * (MEM_DIM ** -0.5)
            e = jnp.exp(s - jnp.max(s, axis=-1, keepdims=True))
            p = e / jnp.sum(e, axis=-1, keepdims=True)
            outs.append(_dot(p.astype(BF16), v2))
        o_ref[:, cols] = jnp.where(lo_head, outs[0], outs[1])


def _mem_attn(proj, q_col_block, kv, batch, seq):
    n = proj.shape[0]
    tq = min(MEM_ROWS, seq)
    nt = seq // tq
    return pl.pallas_call(
        _mem_attn_kernel,
        grid=(batch, nt),
        in_specs=[pl.BlockSpec((tq, MEM_W), lambda b, t: (b * nt + t, q_col_block)),
                  pl.BlockSpec((N_MEM, 2 * MEM_W), lambda b, t: (b, 0))],
        out_specs=pl.BlockSpec((tq, MEM_W), lambda b, t: (b * nt + t, 0)),
        out_shape=jax.ShapeDtypeStruct((n, MEM_W), F32),
        compiler_params=_params(("parallel", "parallel")),
        name="mem_attn",
    )(proj, kv)


SB_TILE = 256


SB_MASKED = -1e30


def _sb_kernel(q_ref, k_ref, v_ref, o_ref, *scratch):
    acc_refs, run_refs = scratch[0:2], scratch[2:4]
    stage = (scratch[4:10], scratch[10:16])
    i = pl.program_id(2)
    t = q_ref.shape[0]
    heads = range(2)
    row = lax.broadcasted_iota(I32, (t, t), 0)
    col = lax.broadcasted_iota(I32, (t, t), 1)
    later = (row > col).astype(BF16)
    lo_head = lax.broadcasted_iota(I32, (t, LANES), 1) < SB_DIM
    q2 = q_ref[...] * (SB_DIM ** -0.5)
    qa = (jnp.where(lo_head, q2, 0.0).astype(BF16), jnp.where(lo_head, 0.0, q2).astype(BF16))
    for ref in acc_refs + run_refs:
        ref[...] = jnp.zeros_like(ref)

    def scores(j, p, diag):
        kj = k_ref[pl.ds(pl.multiple_of(j * t, t), t), :]
        z = [lax.dot_general(qa[a], kj, NT_DIMS, preferred_element_type=F32) for a in heads]
        for a in heads:
            ls = jnp.minimum(z[a], 0.0) - jnp.log(1.0 + jnp.exp(-jnp.abs(z[a])))
            ln = ls - z[a]
            if diag:
                visible = col < row
                ln = jnp.where(visible, ln, 0.0)
                ls = jnp.where(visible, ls, SB_MASKED)
            hi, lo = _split_bf16(ln, 2)
            stage[p][a][...] = ls
            stage[p][2 + a][...] = hi
            stage[p][4 + a][...] = lo

    def apply(j, p):
        vj = v_ref[pl.ds(pl.multiple_of(j * t, t), t), :]
        hi = [stage[p][2 + a][...] for a in heads]
        lo = [stage[p][4 + a][...] for a in heads]
        exc = [_dot(hi[a], later) + _dot(lo[a], later) for a in heads]
        run = [run_refs[a][...] for a in heads]
        a_w = [jnp.exp(stage[p][a][...] + exc[a] + jnp.concatenate([run[a]] * (t // LANES), axis=1))
               for a in heads]
        out = [_dot(a_w[a].astype(BF16), vj) for a in heads]
        for a in heads:
            acc_refs[a][...] += out[a]
            total = exc[a][:, 0:1] + hi[a][:, 0:1].astype(F32) + lo[a][:, 0:1].astype(F32)
            run_refs[a][...] = run[a] + jnp.broadcast_to(total, (t, LANES))

    scores(i, 0, True)

    def two_steps(m, carry):
        j = i - 2 * m
        scores(j - 1, 1, False)
        apply(j, 0)
        scores(j - 2, 0, False)
        apply(j - 1, 1)
        return carry

    lax.fori_loop(0, i // 2, two_steps, 0)

    @pl.when(i % 2 == 1)
    def _():
        scores(0, 1, False)
        apply(1, 0)
        apply(0, 1)

    @pl.when(i % 2 == 0)
    def _():
        apply(0, 0)

    o_ref[...] = jnp.where(lo_head, acc_refs[0][...], acc_refs[1][...])


def _stick_breaking(qproj, kv, batch, seq):
    n = qproj.shape[0]
    t = min(SB_TILE, seq)
    nq = seq // t
    pairs = SB_W // LANES
    return pl.pallas_call(
        _sb_kernel,
        grid=(batch, pairs, nq),
        in_specs=[pl.BlockSpec((t, LANES), lambda b, j, i: (b * nq + i, j)),
                  pl.BlockSpec((seq, LANES), lambda b, j, i: (b, j)),
                  pl.BlockSpec((seq, LANES), lambda b, j, i: (b, pairs + j))],
        out_specs=pl.BlockSpec((t, LANES), lambda b, j, i: (b * nq + i, j)),
        out_shape=jax.ShapeDtypeStruct((n, SB_W), F32),
        scratch_shapes=([pltpu.VMEM((t, LANES), F32)] * 4
                        + ([pltpu.VMEM((t, t), F32)] * 2 + [pltpu.VMEM((t, t), BF16)] * 4) * 2),
        compiler_params=_params(("parallel", "parallel", "arbitrary")),
        name="stick_breaking",
    )(qproj, kv, kv)


LN_ROWS = 256


def _proj_ln_kernel(a1_ref, a2_ref, h_ref, w1_ref, w2_ref, g_ref, b_ref, o_ref, ob_ref, xw_ref):
    mix = _dot(a1_ref[...].astype(BF16), w1_ref[...]) + _dot(a2_ref[...].astype(BF16), w2_ref[...])
    y = _layer_norm(DEEPNORM_ALPHA * h_ref[...] + mix, g_ref[...], b_ref[...])
    o_ref[...] = y
    ob_ref[...] = y.astype(BF16)
    _store_packed_rows(y, xw_ref)


def _ln_out(h, tm):
    spec = pl.BlockSpec((tm, h.shape[1]), lambda i: (i, 0))
    return [spec, spec], [jax.ShapeDtypeStruct(h.shape, F32), jax.ShapeDtypeStruct(h.shape, BF16)]


def _proj_ln(a1, a2, h, w1, w2, g, b):
    n = h.shape[0]
    tm = min(LN_ROWS, n)
    full = lambda arr: pl.BlockSpec(arr.shape, lambda i: (0, 0))
    rows = lambda arr: pl.BlockSpec((tm, arr.shape[1]), lambda i: (i, 0))
    out_specs, out_shape = _ln_out(h, tm)
    out_specs = out_specs + [pl.BlockSpec((tm * ROW_WORDS, LANES), lambda i: (i, 0))]
    out_shape = out_shape + [jax.ShapeDtypeStruct((n * ROW_WORDS, LANES), I32)]
    return pl.pallas_call(
        _proj_ln_kernel,
        grid=(n // tm,),
        in_specs=[rows(a1), rows(a2), rows(h), full(w1), full(w2), full(g), full(b)],
        out_specs=out_specs,
        out_shape=out_shape,
        compiler_params=_params(("parallel",)),
        name="proj_ln",
    )(a1, a2, h, w1, w2, g, b)


def _res_ln_kernel(f_ref, h_ref, g_ref, b_ref, o_ref, ob_ref):
    tm = h_ref.shape[0]
    f = jnp.concatenate([f_ref[pl.ds(c, tm, stride=ROW_CHUNKS), :] for c in range(ROW_CHUNKS)], axis=1)
    y = _layer_norm(DEEPNORM_ALPHA * h_ref[...] + f, g_ref[...], b_ref[...])
    o_ref[...] = y
    ob_ref[...] = y.astype(BF16)


def _res_ln(f, h, g, b):
    n = h.shape[0]
    tm = min(512, n)
    full = lambda arr: pl.BlockSpec(arr.shape, lambda i: (0, 0))
    rows = lambda arr: pl.BlockSpec((tm, arr.shape[1]), lambda i: (i, 0))
    out_specs, out_shape = _ln_out(h, tm)
    return pl.pallas_call(
        _res_ln_kernel,
        grid=(n // tm,),
        in_specs=[pl.BlockSpec((tm * ROW_CHUNKS, LANES), lambda i: (i, 0)), rows(h), full(g), full(b)],
        out_specs=out_specs,
        out_shape=out_shape,
        compiler_params=_params(("parallel",)),
        name="res_ln",
    )(f, h, g, b)


TOPK_TOKENS = 1024


def _top16_rows(problems):
    state = []
    for vals, payload in problems:
        k = vals.shape[0]
        rows = lax.broadcasted_iota(I32, (k // 2, vals.shape[1]), 0).astype(F32)
        first, second = vals[:k // 2], vals[k // 2:]
        keep = first >= second
        slot = {"hi": jnp.where(keep, first, second), "lo": jnp.where(keep, second, first),
                "row_hi": jnp.where(keep, rows, rows + float(k // 2)),
                "row_lo": jnp.where(keep, rows + float(k // 2), rows), "k": float(k)}
        if payload is not None:
            slot["pay_hi"] = jnp.where(keep, payload[:k // 2], payload[k // 2:])
            slot["pay_lo"] = jnp.where(keep, payload[k // 2:], payload[:k // 2])
        state.append(slot)
    outs = [([], []) for _ in problems]
    for _ in range(PEER_TOPK):
        for p, slot in enumerate(state):
            m = jnp.max(slot["hi"], axis=0, keepdims=True)
            tied_rows = jnp.where(slot["hi"] == m, slot["row_hi"], slot["k"])
            am = jnp.min(tied_rows, axis=0, keepdims=True)
            hit = tied_rows == am
            outs[p][0].append(m)
            if "pay_hi" in slot:
                outs[p][1].append(jnp.max(jnp.where(hit, slot["pay_hi"], -1.0), axis=0, keepdims=True))
                slot["pay_hi"] = jnp.where(hit, slot["pay_lo"], slot["pay_hi"])
            else:
                outs[p][1].append(am)
            slot["hi"] = jnp.where(hit, slot["lo"], slot["hi"])
            slot["row_hi"] = jnp.where(hit, slot["row_lo"], slot["row_hi"])
            slot["lo"] = jnp.where(hit, -jnp.inf, slot["lo"])
    return outs


def _pair_candidates(top0, top1):
    (s0, i0), (s1, i1) = top0, top1
    s0_all = jnp.concatenate(s0, axis=0)
    i0_all = jnp.concatenate(i0, axis=0)
    s1_all = jnp.concatenate(s1, axis=0)
    i1_all = jnp.concatenate(i1, axis=0)
    half = PEER_TOPK // 2
    sub = lax.broadcasted_iota(I32, (half, s0_all.shape[1]), 0)
    cand_s = [s0[0] + s1_all]
    cand_i = [i0[0] * float(PEER_KEYS) + i1_all]
    for i in range(1, half):
        cand_s.append(jnp.where(sub < PEER_TOPK // (i + 1), s0[i] + s1_all[:half], -jnp.inf))
        cand_i.append(i0[i] * float(PEER_KEYS) + i1_all[:half])
    cand_s.append(s0_all[half:] + s1[0])
    cand_i.append(i0_all[half:] * float(PEER_KEYS) + i1[0])
    return jnp.concatenate(cand_s, axis=0), jnp.concatenate(cand_i, axis=0)


def _peer_topk_kernel(q_ref, sk_ref, idx_ref, g_ref):
    groups = q_ref.shape[0] // LANES
    scores = []
    for grp in range(groups):
        for p in range(2):
            qp = q_ref[grp * LANES:(grp + 1) * LANES, p * PEER_QHALF:(p + 1) * PEER_QHALF]
            s_t = lax.dot_general(sk_ref[0, p], qp.astype(BF16), NT_DIMS, preferred_element_type=F32)
            scores.append((s_t, None))
    tops = _top16_rows(scores)
    best = _top16_rows([_pair_candidates(tops[2 * grp], tops[2 * grp + 1]) for grp in range(groups)])
    for grp, (best_s, best_i) in enumerate(best):
        cols = slice(grp * LANES, (grp + 1) * LANES)
        e = [jnp.exp(s - best_s[0]) for s in best_s]
        g_ref[:, cols] = jnp.concatenate(e, axis=0) / sum(e)
        idx_ref[:, cols] = (jnp.concatenate(best_i, axis=0) * float(ROW_WORDS)).astype(I32)


def _peer_topk(qp, subkeys):
    n = qp.shape[0]
    tt = min(TOPK_TOKENS, n)
    spec_out = pl.BlockSpec((PEER_TOPK, tt), lambda i, h: (h, i))
    return pl.pallas_call(
        _peer_topk_kernel,
        grid=(n // tt, PEER_HEADS),
        in_specs=[pl.BlockSpec((tt, 2 * PEER_QHALF), lambda i, h: (i, h)),
                  pl.BlockSpec((1, 2, PEER_KEYS, PEER_QHALF), lambda i, h: (h, 0, 0, 0))],
        out_specs=[spec_out, spec_out],
        out_shape=[jax.ShapeDtypeStruct((PEER_PICKS, n), I32),
                   jax.ShapeDtypeStruct((PEER_PICKS, n), F32)],
        compiler_params=_params(("parallel", "parallel")),
        name="peer_topk",
    )(qp, subkeys)


PACK_ROWS = 512


def _store_packed_rows(x, o_ref):
    r = x.shape[0]
    bits = lax.bitcast_convert_type(x.astype(BF16).astype(F32), I32)
    for s in range(ROW_WORDS):
        lo = lax.shift_right_logical(bits[:, (2 * s) * LANES:(2 * s + 1) * LANES], 16)
        hi = bits[:, (2 * s + 1) * LANES:(2 * s + 2) * LANES]
        o_ref[pl.ds(s, r, stride=ROW_WORDS), :] = hi | lo


def _pack_table_kernel(x_ref, o_ref):
    _store_packed_rows(x_ref[0], o_ref)


def _pack_table(tabs, layer):
    e = tabs.shape[1]
    tr = min(PACK_ROWS, e)
    return pl.pallas_call(
        _pack_table_kernel,
        grid=(e // tr,),
        in_specs=[pl.BlockSpec((1, tr, D_MODEL), lambda i: (layer, i, 0))],
        out_specs=pl.BlockSpec((tr * ROW_WORDS, LANES), lambda i: (i, 0)),
        out_shape=jax.ShapeDtypeStruct((e * ROW_WORDS, LANES), I32),
        compiler_params=_params(("parallel",)),
        name="pack_table",
    )(tabs)


PEER_TOKENS = 128
PEER_UNROLL = 64
PAIRS = PEER_PICKS // 2


def _gather_rows(idx_ref, tab_ref, n):
    pieces = []
    for k in range(PAIRS):
        e0 = idx_ref[0, n, 2 * k]
        e1 = idx_ref[0, n, 2 * k + 1]
        words = jnp.concatenate(
            [tab_ref[pl.ds(pl.multiple_of(e, ROW_WORDS), ROW_WORDS), :] for e in (e0, e1)],
            axis=0)
        pieces.append(pltpu.bitcast(words, BF16))
    return pieces


def _token_loop(n_tokens, group):
    def body(m, carry):
        for q in range(PEER_UNROLL // SUBLANES):
            group(pl.multiple_of(m * PEER_UNROLL + q * SUBLANES, SUBLANES))
        return carry

    lax.fori_loop(0, n_tokens // PEER_UNROLL, body, 0)


def _peer_u_kernel(idx_ref, x_ref, tab_ref, o_ref):
    ones = jnp.ones((LANES, LANES), BF16)
    lane = lax.broadcasted_iota(I32, (ROW_CHUNKS, LANES), 1)

    def token(n):
        x4 = x_ref[pl.ds(pl.multiple_of(n * ROW_WORDS, ROW_WORDS), ROW_WORDS), :]
        xw = pltpu.bitcast(jnp.concatenate([x4, x4], axis=0), BF16)
        prods = [piece * xw for piece in _gather_rows(idx_ref, tab_ref, n)]
        z = _dot(jnp.concatenate(prods, axis=0), ones)
        n_acc = 4
        acc = [jnp.zeros((ROW_CHUNKS, LANES), F32) for _ in range(n_acc)]
        for e in range(PEER_PICKS):
            acc[e % n_acc] = acc[e % n_acc] + jnp.where(
                lane == e, z[e * ROW_CHUNKS:(e + 1) * ROW_CHUNKS, :], 0.0)
        return jnp.sum(sum(acc), axis=0, keepdims=True)

    def group(base):
        o_ref[pl.ds(base, SUBLANES), :] = jnp.concatenate(
            [token(base + r) for r in range(SUBLANES)], axis=0)

    _token_loop(o_ref.shape[0], group)


def _peer_v_kernel(idx_ref, wexp_ref, tab_ref, o_ref):
    lane = lax.broadcasted_iota(I32, (ROW_CHUNKS, LANES), 1)
    sub = lax.broadcasted_iota(I32, (ROW_CHUNKS, LANES), 0)
    own_chunk = (lane & (ROW_CHUNKS - 1)) == sub

    def group(base):
        weights = [wexp_ref[j, pl.ds(base, SUBLANES), :] for j in range(ROW_CHUNKS)]
        for r in range(SUBLANES):
            n = base + r
            rows = jnp.concatenate(_gather_rows(idx_ref, tab_ref, n), axis=0)
            lhs = jnp.concatenate(
                [jnp.where(own_chunk, jnp.broadcast_to(weights[j][r:r + 1, :], (ROW_CHUNKS, LANES)), 0.0)
                 for j in range(ROW_CHUNKS)], axis=1).astype(BF16)
            o_ref[n] = _dot(lhs, rows)

    _token_loop(o_ref.shape[0], group)


def _idx_spec(tb):
    return pl.BlockSpec((1, tb, PEER_PICKS), lambda i: (i, 0, 0), memory_space=pltpu.SMEM)


def _table_spec(tab):
    return pl.BlockSpec(tab.shape, lambda i: (0, 0), pipeline_mode=pl.Buffered(1))


def _peer_u(idx, xw, tab):
    n = idx.shape[0]
    tb = min(PEER_TOKENS, n)
    return pl.pallas_call(
        _peer_u_kernel,
        grid=(n // tb,),
        in_specs=[_idx_spec(tb), pl.BlockSpec((tb * ROW_WORDS, LANES), lambda i: (i, 0)),
                  _table_spec(tab)],
        out_specs=pl.BlockSpec((tb, LANES), lambda i: (i, 0)),
        out_shape=jax.ShapeDtypeStruct((n, LANES), F32),
        compiler_params=_params(("arbitrary",), VMEM_LIMIT_TABLE),
        name="peer_u",
    )(idx.reshape(n // tb, tb, PEER_PICKS), xw, tab)


def _peer_v(idx, wexp, tab):
    n = idx.shape[0]
    tb = min(PEER_TOKENS, n)
    out = pl.pallas_call(
        _peer_v_kernel,
        grid=(n // tb,),
        in_specs=[_idx_spec(tb),
                  pl.BlockSpec((ROW_CHUNKS, tb, LANES), lambda i: (0, i, 0)),
                  _table_spec(tab)],
        out_specs=pl.BlockSpec((tb, ROW_CHUNKS, LANES), lambda i: (i, 0, 0)),
        out_shape=jax.ShapeDtypeStruct((n, ROW_CHUNKS, LANES), F32),
        compiler_params=_params(("arbitrary",), VMEM_LIMIT_TABLE),
        name="peer_v",
    )(idx.reshape(n // tb, tb, PEER_PICKS), wexp, tab)
    return out.reshape(n * ROW_CHUNKS, LANES)


def _gate_act_kernel(a_ref, g_ref, e_ref, o_ref):
    a = a_ref[...]
    w = g_ref[...] * (0.5 * a * (1.0 + lax.erf(a * (2.0 ** -0.5))))
    wx = _dot(w.astype(BF16), e_ref[...])
    for j in range(ROW_CHUNKS):
        o_ref[j] = wx[:, j * LANES:(j + 1) * LANES]


def _gate_act(act, g):
    n = act.shape[0]
    tm = min(1024, n)
    spec = pl.BlockSpec((tm, LANES), lambda i: (i, 0))
    expand = (jnp.arange(D_MODEL)[None, :] // ROW_CHUNKS == jnp.arange(LANES)[:, None]).astype(BF16)
    return pl.pallas_call(
        _gate_act_kernel,
        grid=(n // tm,),
        in_specs=[spec, spec, pl.BlockSpec((LANES, D_MODEL), lambda i: (0, 0))],
        out_specs=pl.BlockSpec((ROW_CHUNKS, tm, LANES), lambda i: (0, i, 0)),
        out_shape=jax.ShapeDtypeStruct((ROW_CHUNKS, n, LANES), F32),
        compiler_params=_params(("parallel",)),
        name="peer_gate_act",
    )(act, g, expand)


def _peer(hb, xw, w_q, subkeys, u_tabs, v_tabs, layer):
    qp = _matmul(hb, w_q.astype(BF16))
    idx_t, g_t = _peer_topk(qp, subkeys.astype(BF16))
    idx = idx_t.T
    act = _peer_u(idx, xw, _pack_table(u_tabs, layer))
    return _peer_v(idx, _gate_act(act, g_t.T), _pack_table(v_tabs, layer))


def kernel(x, mem, a_w_in, a_w_gate2, a_b_gate, a_head_g, a_w_mem_kv, a_w_out, b_w_in, b_w_mem_kv,
           b_w_out, sb_w_kv, peer_w_q, peer_subkeys, peer_u, peer_v, ln_g, ln_b):
    batch, seq, d = x.shape
    n = batch * seq
    h = x.reshape(n, d)
    memf = mem.reshape(batch * N_MEM, d)
    pad_w = 2 * GLA_DV
    s = [GLA_K, 2 * GLA_K, 2 * GLA_K + GLA_V, 2 * GLA_K + 2 * GLA_V,
         2 * GLA_K + 2 * GLA_V + GLA_GATE_RANK]

    w_in = a_w_in[0]
    w_in_r = jnp.concatenate(
        [w_in[:, :s[3]], w_in[:, s[4]:], w_in[:, s[3]:s[4]],
         jnp.zeros((d, pad_w - GLA_GATE_RANK), F32)], axis=1).astype(BF16)
    proj = _matmul(h, w_in_r)
    wg = jnp.concatenate([a_w_gate2[0], jnp.zeros((pad_w - GLA_GATE_RANK, GLA_K), F32)],
                         axis=0).astype(BF16)
    o = _gla(proj, wg, a_b_gate[0].reshape(1, GLA_K), a_head_g[0].reshape(1, GLA_V), batch, seq)
    kv_mem = _matmul(memf, a_w_mem_kv[0].astype(BF16))
    m = _mem_attn(proj, (2 * GLA_K + 2 * GLA_V) // MEM_W, kv_mem, batch, seq)
    w_out = a_w_out[0].astype(BF16)
    h, hb, xw = _proj_ln(o, m, h, w_out[:GLA_V], w_out[GLA_V:], ln_g[0, 0].reshape(1, d),
                         ln_b[0, 0].reshape(1, d))
    ffn = _peer(hb, xw, peer_w_q[0], peer_subkeys[0], peer_u, peer_v, 0)
    h, hb = _res_ln(ffn, h, ln_g[0, 1].reshape(1, d), ln_b[0, 1].reshape(1, d))

    kv_sb = _matmul(hb, sb_w_kv.astype(BF16), out_dtype=BF16)
    proj = _matmul(hb, b_w_in[0].astype(BF16))
    o = _stick_breaking(proj, kv_sb, batch, seq)
    kv_mem = _matmul(memf, b_w_mem_kv[0].astype(BF16))
    m = _mem_attn(proj, SB_W // MEM_W, kv_mem, batch, seq)
    w_out = b_w_out[0].astype(BF16)
    h, hb, xw = _proj_ln(o, m, h, w_out[:SB_W], w_out[SB_W:], ln_g[1, 0].reshape(1, d),
                         ln_b[1, 0].reshape(1, d))
    ffn = _peer(hb, xw, peer_w_q[1], peer_subkeys[1], peer_u, peer_v, 1)
    h, _ = _res_ln(ffn, h, ln_g[1, 1].reshape(1, d), ln_b[1, 1].reshape(1, d))
    return h.reshape(batch, seq, d)
```

```python
import functools

import jax
import jax.numpy as jnp
from jax import lax
from jax.experimental import pallas as pl
from jax.experimental.pallas import tpu as pltpu

F32 = jnp.float32
BF16 = jnp.bfloat16
I32 = jnp.int32

LANES = 128
SUBLANES = 8
VMEM_LIMIT_DEFAULT = 48 * 1024 * 1024
VMEM_LIMIT_TABLE = 56 * 1024 * 1024

D_MODEL = 1024
N_MEM = 256
GLA_HEADS = 6
GLA_DK = 64
GLA_DV = 128
GLA_K = GLA_HEADS * GLA_DK
GLA_V = GLA_HEADS * GLA_DV
GLA_GATE_RANK = 16
GLA_TAU = 16.0
GLA_CHUNK = 64
SB_HEADS = 12
SB_DIM = 64
SB_W = SB_HEADS * SB_DIM
SB_BLOCK = 128
MEM_HEADS = 4
MEM_DIM = 64
MEM_W = MEM_HEADS * MEM_DIM
PEER_HEADS = 8
PEER_KEYS = 128
PEER_EXPERTS = PEER_KEYS * PEER_KEYS
PEER_TOPK = 16
PEER_QHALF = 128
PEER_PICKS = PEER_HEADS * PEER_TOPK
DEPTH = 2
DEEPNORM_ALPHA = (2.0 * DEPTH) ** 0.25
EPS = 1e-5

ROW_CHUNKS = D_MODEL // LANES
ROW_WORDS = ROW_CHUNKS // 2

NT_DIMS = (((1,), (1,)), ((), ()))
TN_DIMS = (((0,), (0,)), ((), ()))


def _params(semantics, vmem=VMEM_LIMIT_DEFAULT):
    return pltpu.CompilerParams(dimension_semantics=semantics, vmem_limit_bytes=vmem)


def _log_sigmoid(z):
    return jnp.minimum(z, 0.0) - jnp.log1p(jnp.exp(-jnp.abs(z)))


def _dot(a, b):
    return jnp.dot(a, b, preferred_element_type=F32)


def _split_bf16(x, parts):
    out = []
    r = x
    for _ in range(parts):
        p = r.astype(BF16)
        out.append(p)
        r = r - p.astype(F32)
    return out


def _layer_norm(y, g, b):
    mu = jnp.mean(y, axis=-1, keepdims=True)
    yc = y - mu
    var = jnp.mean(yc * yc, axis=-1, keepdims=True)
    return yc * lax.rsqrt(var + EPS) * g + b


def _matmul_kernel(x_ref, w_ref, o_ref):
    o_ref[...] = _dot(x_ref[...].astype(BF16), w_ref[...]).astype(o_ref.dtype)


MATMUL_ROWS = 512


def _matmul(x, w, out_dtype=F32):
    m, k = x.shape
    n = w.shape[1]
    tm = min(MATMUL_ROWS, m)
    return pl.pallas_call(
        _matmul_kernel,
        grid=(m // tm,),
        in_specs=[pl.BlockSpec((tm, k), lambda i: (i, 0)),
                  pl.BlockSpec((k, n), lambda i: (0, 0))],
        out_specs=pl.BlockSpec((tm, n), lambda i: (i, 0)),
        out_shape=jax.ShapeDtypeStruct((m, n), out_dtype),
        compiler_params=_params(("parallel",)),
        name="matmul",
    )(x, w)


GLA_ROWS = 256


def _gla_kernel(q_ref, k_ref, v_ref, r_ref, glr_ref, wg_ref, bg_ref, hg_ref, o_ref, st_ref):
    @pl.when(pl.program_id(2) == 0)
    def _():
        st_ref[...] = jnp.zeros_like(st_ref)

    c_rows = GLA_CHUNK
    lane = lax.broadcasted_iota(I32, (c_rows, LANES), 1)
    head_masks = (lane < GLA_DK, lane >= GLA_DK)
    row = lax.broadcasted_iota(I32, (c_rows, c_rows), 0)
    col = lax.broadcasted_iota(I32, (c_rows, c_rows), 1)
    causal = row >= col
    tril = causal.astype(BF16)
    wg = wg_ref[...]
    bg = bg_ref[...]
    hg = hg_ref[...]
    chunks = range(q_ref.shape[0] // c_rows)
    heads = range(2)
    rows = [slice(c * c_rows, (c + 1) * c_rows) for c in chunks]
    cols = [slice(a * GLA_DV, (a + 1) * GLA_DV) for a in heads]
    z = [_dot(glr_ref[rows[c], :].astype(BF16), wg) + bg for c in chunks]
    lg_parts = [_split_bf16(_log_sigmoid(z[c]) * (1.0 / GLA_TAU), 3) for c in chunks]
    bc = [sum(_dot(tril, part) for part in lg_parts[c]) for c in chunks]
    bl = [bc[c][c_rows - 1:c_rows, :] for c in chunks]
    qd = [q_ref[rows[c], :] * (GLA_DK ** -0.5) * jnp.exp(bc[c]) for c in chunks]
    ki = [(k_ref[rows[c], :] * jnp.exp(-bc[c])).astype(BF16) for c in chunks]
    ke = [k_ref[rows[c], :] * jnp.exp(bl[c] - bc[c]) for c in chunks]
    dec = [jnp.exp(bl[c]) for c in chunks]
    qa = [[jnp.where(head_masks[a], qd[c], 0.0).astype(BF16) for a in heads] for c in chunks]
    kea = [[jnp.where(head_masks[a], ke[c], 0.0).astype(BF16) for a in heads] for c in chunks]
    va = [[v_ref[rows[c], cols[a]].astype(BF16) for a in heads] for c in chunks]
    attn = [[jnp.where(causal, lax.dot_general(qa[c][a], ki[c], NT_DIMS, preferred_element_type=F32),
                       0.0).astype(BF16) for a in heads] for c in chunks]
    inc = [[lax.dot_general(va[c][a], kea[c][a], TN_DIMS, preferred_element_type=F32)
            for a in heads] for c in chunks]
    o_intra = [[_dot(attn[c][a], va[c][a]) for a in heads] for c in chunks]
    states = [st_ref[0], st_ref[1]]
    for c in chunks:
        for a in heads:
            o = o_intra[c][a] + lax.dot_general(qa[c][a], states[a].astype(BF16), NT_DIMS,
                                                preferred_element_type=F32)
            states[a] = states[a] * dec[c] + inc[c][a]
            ms = jnp.mean(o * o, axis=-1, keepdims=True)
            r = r_ref[rows[c], cols[a]]
            silu = r / (1.0 + jnp.exp(-r))
            o_ref[rows[c], cols[a]] = o * lax.rsqrt(ms + EPS) * hg[:, cols[a]] * silu
    st_ref[0] = states[0]
    st_ref[1] = states[1]


def _gla(proj, wg, bg, hg, batch, seq):
    n = proj.shape[0]
    tr = min(GLA_ROWS, seq)
    nt = seq // tr
    rowmap = lambda b, j, t: b * nt + t
    pair_w = 2 * GLA_DV
    return pl.pallas_call(
        _gla_kernel,
        grid=(batch, GLA_HEADS // 2, nt),
        in_specs=[
            pl.BlockSpec((tr, LANES), lambda b, j, t: (rowmap(b, j, t), j)),
            pl.BlockSpec((tr, LANES), lambda b, j, t: (rowmap(b, j, t), GLA_K // LANES + j)),
            pl.BlockSpec((tr, pair_w), lambda b, j, t: (rowmap(b, j, t), 2 * GLA_K // pair_w + j)),
            pl.BlockSpec((tr, pair_w),
                         lambda b, j, t: (rowmap(b, j, t), (2 * GLA_K + GLA_V) // pair_w + j)),
            pl.BlockSpec((tr, pair_w),
                         lambda b, j, t: (rowmap(b, j, t), (2 * GLA_K + 2 * GLA_V + MEM_W) // pair_w)),
            pl.BlockSpec((pair_w, LANES), lambda b, j, t: (0, j)),
            pl.BlockSpec((1, LANES), lambda b, j, t: (0, j)),
            pl.BlockSpec((1, pair_w), lambda b, j, t: (0, j)),
        ],
        out_specs=pl.BlockSpec((tr, pair_w), lambda b, j, t: (rowmap(b, j, t), j)),
        out_shape=jax.ShapeDtypeStruct((n, GLA_V), F32),
        scratch_shapes=[pltpu.VMEM((2, GLA_DV, LANES), F32)],
        compiler_params=_params(("parallel", "parallel", "arbitrary")),
        name="gla",
    )(proj, proj, proj, proj, proj, wg, bg, hg)


MEM_ROWS = 512


def _mem_attn_kernel(q_ref, kv_ref, o_ref):
    tq = q_ref.shape[0]
    lane = lax.broadcasted_iota(I32, (tq, LANES), 1)
    lo_head = lane < MEM_DIM
    for j in range(MEM_W // LANES):
        cols = slice(j * LANES, (j + 1) * LANES)
        q2 = q_ref[:, cols]
        k2 = kv_ref[:, cols].astype(BF16)
        v2 = kv_ref[:, MEM_W + j * LANES:MEM_W + (j + 1) * LANES].astype(BF16)
        outs = []
        for a in range(2):
            qa = jnp.where(lo_head if a == 0 else ~lo_head, q2, 0.0).astype(BF16)
            s = lax.dot_general(qa, k2, NT_DIMS, preferred_element_type=F32) * (MEM_DIM ** -0.5)
            e = jnp.exp(s - jnp.max(s, axis=-1, keepdims=True))
            p = e / jnp.sum(e, axis=-1, keepdims=True)
            outs.append(_dot(p.astype(BF16), v2))
        o_ref[:, cols] = jnp.where(lo_head, outs[0], outs[1])


def _mem_attn(proj, q_col_block, kv, batch, seq):
    n = proj.shape[0]
    tq = min(MEM_ROWS, seq)
    nt = seq // tq
    return pl.pallas_call(
        _mem_attn_kernel,
        grid=(batch, nt),
        in_specs=[pl.BlockSpec((tq, MEM_W), lambda b, t: (b * nt + t, q_col_block)),
                  pl.BlockSpec((N_MEM, 2 * MEM_W), lambda b, t: (b, 0))],
        out_specs=pl.BlockSpec((tq, MEM_W), lambda b, t: (b * nt + t, 0)),
        out_shape=jax.ShapeDtypeStruct((n, MEM_W), F32),
        compiler_params=_params(("parallel", "parallel")),
        name="mem_attn",
    )(proj, kv)


SB_TILE = 256


SB_MASKED = -1e30


def _sb_kernel(q_ref, k_ref, v_ref, o_ref, *scratch):
    acc_refs, run_refs = scratch[0:2], scratch[2:4]
    stage = (scratch[4:10], scratch[10:16])
    i = pl.program_id(2)
    t = q_ref.shape[0]
    heads = range(2)
    row = lax.broadcasted_iota(I32, (t, t), 0)
    col = lax.broadcasted_iota(I32, (t, t), 1)
    later = (row > col).astype(BF16)
    lo_head = lax.broadcasted_iota(I32, (t, LANES), 1) < SB_DIM
    q2 = q_ref[...] * (SB_DIM ** -0.5)
    qa = (jnp.where(lo_head, q2, 0.0).astype(BF16), jnp.where(lo_head, 0.0, q2).astype(BF16))
    for ref in acc_refs + run_refs:
        ref[...] = jnp.zeros_like(ref)

    def qk(j):
        kj = k_ref[pl.ds(pl.multiple_of(j * t, t), t), :]
        return [lax.dot_general(qa[a], kj, NT_DIMS, preferred_element_type=F32) for a in heads]

    def scores(z, p, diag):
        for a in heads:
            ls = jnp.minimum(z[a], 0.0) - jnp.log(1.0 + jnp.exp(-jnp.abs(z[a])))
            ln = ls - z[a]
            if diag:
                visible = col < row
                ln = jnp.where(visible, ln, 0.0)
                ls = jnp.where(visible, ls, SB_MASKED)
            hi, lo = _split_bf16(ln, 2)
            stage[p][a][...] = ls
            stage[p][2 + a][...] = hi
            stage[p][4 + a][...] = lo

    def suffix(p):
        hi = [stage[p][2 + a][...] for a in heads]
        lo = [stage[p][4 + a][...] for a in heads]
        return [_dot(hi[a], later) + _dot(lo[a], later) for a in heads], hi, lo

    def apply(j, p, sums):
        exc, hi, lo = sums
        vj = v_ref[pl.ds(pl.multiple_of(j * t, t), t), :]
        run = [run_refs[a][...] for a in heads]
        a_w = [jnp.exp(stage[p][a][...] + exc[a] + jnp.concatenate([run[a]] * (t // LANES), axis=1))
               for a in heads]
        out = [_dot(a_w[a].astype(BF16), vj) for a in heads]
        for a in heads:
            acc_refs[a][...] += out[a]
            total = exc[a][:, 0:1] + hi[a][:, 0:1].astype(F32) + lo[a][:, 0:1].astype(F32)
            run_refs[a][...] = run[a] + jnp.broadcast_to(total, (t, LANES))

    scores(qk(i), 0, True)

    def two_steps(m, carry):
        j = i - 2 * m
        z1 = qk(j - 1)
        sums0 = suffix(0)
        z2 = qk(j - 2)
        scores(z1, 1, False)
        apply(j, 0, sums0)
        sums1 = suffix(1)
        scores(z2, 0, False)
        apply(j - 1, 1, sums1)
        return carry

    lax.fori_loop(0, i // 2, two_steps, 0)

    @pl.when(i % 2 == 1)
    def _():
        z0 = qk(0)
        sums0 = suffix(0)
        scores(z0, 1, False)
        apply(1, 0, sums0)
        apply(0, 1, suffix(1))

    @pl.when(i % 2 == 0)
    def _():
        apply(0, 0, suffix(0))

    o_ref[...] = jnp.where(lo_head, acc_refs[0][...], acc_refs[1][...])


def _stick_breaking(qproj, kv, batch, seq):
    n = qproj.shape[0]
    t = min(SB_TILE, seq)
    nq = seq // t
    pairs = SB_W // LANES
    return pl.pallas_call(
        _sb_kernel,
        grid=(batch, pairs, nq),
        in_specs=[pl.BlockSpec((t, LANES), lambda b, j, i: (b * nq + i, j)),
                  pl.BlockSpec((seq, LANES), lambda b, j, i: (b, j)),
                  pl.BlockSpec((seq, LANES), lambda b, j, i: (b, pairs + j))],
        out_specs=pl.BlockSpec((t, LANES), lambda b, j, i: (b * nq + i, j)),
        out_shape=jax.ShapeDtypeStruct((n, SB_W), F32),
        scratch_shapes=([pltpu.VMEM((t, LANES), F32)] * 4
                        + ([pltpu.VMEM((t, t), F32)] * 2 + [pltpu.VMEM((t, t), BF16)] * 4) * 2),
        compiler_params=_params(("parallel", "parallel", "arbitrary")),
        name="stick_breaking",
    )(qproj, kv, kv)


LN_ROWS = 256


def _proj_ln_kernel(a1_ref, a2_ref, h_ref, w1_ref, w2_ref, g_ref, b_ref, o_ref, ob_ref, xw_ref):
    mix = _dot(a1_ref[...].astype(BF16), w1_ref[...]) + _dot(a2_ref[...].astype(BF16), w2_ref[...])
    y = _layer_norm(DEEPNORM_ALPHA * h_ref[...] + mix, g_ref[...], b_ref[...])
    o_ref[...] = y
    ob_ref[...] = y.astype(BF16)
    _store_packed_rows(y, xw_ref)


def _ln_out(h, tm):
    spec = pl.BlockSpec((tm, h.shape[1]), lambda i: (i, 0))
    return [spec, spec], [jax.ShapeDtypeStruct(h.shape, F32), jax.ShapeDtypeStruct(h.shape, BF16)]


def _proj_ln(a1, a2, h, w1, w2, g, b):
    n = h.shape[0]
    tm = min(LN_ROWS, n)
    full = lambda arr: pl.BlockSpec(arr.shape, lambda i: (0, 0))
    rows = lambda arr: pl.BlockSpec((tm, arr.shape[1]), lambda i: (i, 0))
    out_specs, out_shape = _ln_out(h, tm)
    out_specs = out_specs + [pl.BlockSpec((tm * ROW_WORDS, LANES), lambda i: (i, 0))]
    out_shape = out_shape + [jax.ShapeDtypeStruct((n * ROW_WORDS, LANES), I32)]
    return pl.pallas_call(
        _proj_ln_kernel,
        grid=(n // tm,),
        in_specs=[rows(a1), rows(a2), rows(h), full(w1), full(w2), full(g), full(b)],
        out_specs=out_specs,
        out_shape=out_shape,
        compiler_params=_params(("parallel",)),
        name="proj_ln",
    )(a1, a2, h, w1, w2, g, b)


def _res_ln_kernel(f_ref, h_ref, g_ref, b_ref, o_ref, ob_ref):
    tm = h_ref.shape[0]
    f = jnp.concatenate([f_ref[pl.ds(c, tm, stride=ROW_CHUNKS), :] for c in range(ROW_CHUNKS)], axis=1)
    y = _layer_norm(DEEPNORM_ALPHA * h_ref[...] + f, g_ref[...], b_ref[...])
    o_ref[...] = y
    ob_ref[...] = y.astype(BF16)


def _res_ln(f, h, g, b):
    n = h.shape[0]
    tm = min(512, n)
    full = lambda arr: pl.BlockSpec(arr.shape, lambda i: (0, 0))
    rows = lambda arr: pl.BlockSpec((tm, arr.shape[1]), lambda i: (i, 0))
    out_specs, out_shape = _ln_out(h, tm)
    return pl.pallas_call(
        _res_ln_kernel,
        grid=(n // tm,),
        in_specs=[pl.BlockSpec((tm * ROW_CHUNKS, LANES), lambda i: (i, 0)), rows(h), full(g), full(b)],
        out_specs=out_specs,
        out_shape=out_shape,
        compiler_params=_params(("parallel",)),
        name="res_ln",
    )(f, h, g, b)


TOPK_TOKENS = 1024


def _top16_rows(problems):
    state = []
    for vals, payload in problems:
        k = vals.shape[0]
        rows = lax.broadcasted_iota(I32, (k // 2, vals.shape[1]), 0).astype(F32)
        first, second = vals[:k // 2], vals[k // 2:]
        keep = first >= second
        slot = {"hi": jnp.where(keep, first, second), "lo": jnp.where(keep, second, first),
                "row_hi": jnp.where(keep, rows, rows + float(k // 2)),
                "row_lo": jnp.where(keep, rows + float(k // 2), rows), "k": float(k)}
        if payload is not None:
            slot["pay_hi"] = jnp.where(keep, payload[:k // 2], payload[k // 2:])
            slot["pay_lo"] = jnp.where(keep, payload[k // 2:], payload[:k // 2])
        state.append(slot)
    outs = [([], []) for _ in problems]
    for _ in range(PEER_TOPK):
        for p, slot in enumerate(state):
            m = jnp.max(slot["hi"], axis=0, keepdims=True)
            tied_rows = jnp.where(slot["hi"] == m, slot["row_hi"], slot["k"])
            am = jnp.min(tied_rows, axis=0, keepdims=True)
            hit = tied_rows == am
            outs[p][0].append(m)
            if "pay_hi" in slot:
                outs[p][1].append(jnp.max(jnp.where(hit, slot["pay_hi"], -1.0), axis=0, keepdims=True))
                slot["pay_hi"] = jnp.where(hit, slot["pay_lo"], slot["pay_hi"])
            else:
                outs[p][1].append(am)
            slot["hi"] = jnp.where(hit, slot["lo"], slot["hi"])
            slot["row_hi"] = jnp.where(hit, slot["row_lo"], slot["row_hi"])
            slot["lo"] = jnp.where(hit, -jnp.inf, slot["lo"])
    return outs


def _pair_candidates(top0, top1):
    (s0, i0), (s1, i1) = top0, top1
    s0_all = jnp.concatenate(s0, axis=0)
    i0_all = jnp.concatenate(i0, axis=0)
    s1_all = jnp.concatenate(s1, axis=0)
    i1_all = jnp.concatenate(i1, axis=0)
    half = PEER_TOPK // 2
    sub = lax.broadcasted_iota(I32, (half, s0_all.shape[1]), 0)
    cand_s = [s0[0] + s1_all]
    cand_i = [i0[0] * float(PEER_KEYS) + i1_all]
    for i in range(1, half):
        cand_s.append(jnp.where(sub < PEER_TOPK // (i + 1), s0[i] + s1_all[:half], -jnp.inf))
        cand_i.append(i0[i] * float(PEER_KEYS) + i1_all[:half])
    cand_s.append(s0_all[half:] + s1[0])
    cand_i.append(i0_all[half:] * float(PEER_KEYS) + i1[0])
    return jnp.concatenate(cand_s, axis=0), jnp.concatenate(cand_i, axis=0)


def _peer_topk_kernel(q_ref, sk_ref, idx_ref, g_ref):
    groups = q_ref.shape[0] // LANES
    scores = []
    for grp in range(groups):
        for p in range(2):
            qp = q_ref[grp * LANES:(grp + 1) * LANES, p * PEER_QHALF:(p + 1) * PEER_QHALF]
            s_t = lax.dot_general(sk_ref[0, p], qp.astype(BF16), NT_DIMS, preferred_element_type=F32)
            scores.append((s_t, None))
    tops = _top16_rows(scores)
    best = _top16_rows([_pair_candidates(tops[2 * grp], tops[2 * grp + 1]) for grp in range(groups)])
    for grp, (best_s, best_i) in enumerate(best):
        cols = slice(grp * LANES, (grp + 1) * LANES)
        e = [jnp.exp(s - best_s[0]) for s in best_s]
        g_ref[:, cols] = jnp.concatenate(e, axis=0) / sum(e)
        idx_ref[:, cols] = (jnp.concatenate(best_i, axis=0) * float(ROW_WORDS)).astype(I32)


def _peer_topk(qp, subkeys):
    n = qp.shape[0]
    tt = min(TOPK_TOKENS, n)
    spec_out = pl.BlockSpec((PEER_TOPK, tt), lambda i, h: (h, i))
    return pl.pallas_call(
        _peer_topk_kernel,
        grid=(n // tt, PEER_HEADS),
        in_specs=[pl.BlockSpec((tt, 2 * PEER_QHALF), lambda i, h: (i, h)),
                  pl.BlockSpec((1, 2, PEER_KEYS, PEER_QHALF), lambda i, h: (h, 0, 0, 0))],
        out_specs=[spec_out, spec_out],
        out_shape=[jax.ShapeDtypeStruct((PEER_PICKS, n), I32),
                   jax.ShapeDtypeStruct((PEER_PICKS, n), F32)],
        compiler_params=_params(("parallel", "parallel")),
        name="peer_topk",
    )(qp, subkeys)


PACK_ROWS = 512


def _store_packed_rows(x, o_ref):
    r = x.shape[0]
    bits = lax.bitcast_convert_type(x.astype(BF16).astype(F32), I32)
    for s in range(ROW_WORDS):
        lo = lax.shift_right_logical(bits[:, (2 * s) * LANES:(2 * s + 1) * LANES], 16)
        hi = bits[:, (2 * s + 1) * LANES:(2 * s + 2) * LANES]
        o_ref[pl.ds(s, r, stride=ROW_WORDS), :] = hi | lo


def _pack_table_kernel(x_ref, o_ref):
    _store_packed_rows(x_ref[0], o_ref)


def _pack_table(tabs, layer):
    e = tabs.shape[1]
    tr = min(PACK_ROWS, e)
    return pl.pallas_call(
        _pack_table_kernel,
        grid=(e // tr,),
        in_specs=[pl.BlockSpec((1, tr, D_MODEL), lambda i: (layer, i, 0))],
        out_specs=pl.BlockSpec((tr * ROW_WORDS, LANES), lambda i: (i, 0)),
        out_shape=jax.ShapeDtypeStruct((e * ROW_WORDS, LANES), I32),
        compiler_params=_params(("parallel",)),
        name="pack_table",
    )(tabs)


PEER_TOKENS = 256
PEER_UNROLL = 64
PAIRS = PEER_PICKS // 2


def _gather_rows(idx_ref, tab_ref, n):
    pieces = []
    for k in range(PAIRS):
        e0 = idx_ref[0, n, 2 * k]
        e1 = idx_ref[0, n, 2 * k + 1]
        words = jnp.concatenate(
            [tab_ref[pl.ds(pl.multiple_of(e, ROW_WORDS), ROW_WORDS), :] for e in (e0, e1)],
            axis=0)
        pieces.append(pltpu.bitcast(words, BF16))
    return pieces


def _token_loop(n_tokens, group):
    def body(m, carry):
        for q in range(PEER_UNROLL // SUBLANES):
            group(pl.multiple_of(m * PEER_UNROLL + q * SUBLANES, SUBLANES))
        return carry

    lax.fori_loop(0, n_tokens // PEER_UNROLL, body, 0)


def _peer_u_kernel(idx_ref, x_ref, tab_ref, o_ref):
    ones = jnp.ones((LANES, LANES), BF16)
    lane = lax.broadcasted_iota(I32, (ROW_CHUNKS, LANES), 1)

    def token(n):
        x4 = x_ref[pl.ds(pl.multiple_of(n * ROW_WORDS, ROW_WORDS), ROW_WORDS), :]
        xw = pltpu.bitcast(jnp.concatenate([x4, x4], axis=0), BF16)
        prods = [piece * xw for piece in _gather_rows(idx_ref, tab_ref, n)]
        z = _dot(jnp.concatenate(prods, axis=0), ones)
        n_acc = 4
        acc = [jnp.zeros((ROW_CHUNKS, LANES), F32) for _ in range(n_acc)]
        for e in range(PEER_PICKS):
            acc[e % n_acc] = acc[e % n_acc] + jnp.where(
                lane == e, z[e * ROW_CHUNKS:(e + 1) * ROW_CHUNKS, :], 0.0)
        return jnp.sum(sum(acc), axis=0, keepdims=True)

    def group(base):
        o_ref[pl.ds(base, SUBLANES), :] = jnp.concatenate(
            [token(base + r) for r in range(SUBLANES)], axis=0)

    _token_loop(o_ref.shape[0], group)


def _peer_v_kernel(idx_ref, wexp_ref, tab_ref, o_ref):
    lane = lax.broadcasted_iota(I32, (ROW_CHUNKS, LANES), 1)
    sub = lax.broadcasted_iota(I32, (ROW_CHUNKS, LANES), 0)
    own_chunk = (lane & (ROW_CHUNKS - 1)) == sub

    def group(base):
        weights = [wexp_ref[j, pl.ds(base, SUBLANES), :] for j in range(ROW_CHUNKS)]
        for r in range(SUBLANES):
            n = base + r
            rows = jnp.concatenate(_gather_rows(idx_ref, tab_ref, n), axis=0)
            lhs = jnp.concatenate(
                [jnp.where(own_chunk, jnp.broadcast_to(weights[j][r:r + 1, :], (ROW_CHUNKS, LANES)), 0.0)
                 for j in range(ROW_CHUNKS)], axis=1).astype(BF16)
            o_ref[n] = _dot(lhs, rows)

    _token_loop(o_ref.shape[0], group)


def _idx_spec(tb):
    return pl.BlockSpec((1, tb, PEER_PICKS), lambda i: (i, 0, 0), memory_space=pltpu.SMEM)


def _table_spec(tab):
    return pl.BlockSpec(tab.shape, lambda i: (0, 0), pipeline_mode=pl.Buffered(1))


def _peer_u(idx, xw, tab):
    n = idx.shape[0]
    tb = min(PEER_TOKENS, n)
    return pl.pallas_call(
        _peer_u_kernel,
        grid=(n // tb,),
        in_specs=[_idx_spec(tb), pl.BlockSpec((tb * ROW_WORDS, LANES), lambda i: (i, 0)),
                  _table_spec(tab)],
        out_specs=pl.BlockSpec((tb, LANES), lambda i: (i, 0)),
        out_shape=jax.ShapeDtypeStruct((n, LANES), F32),
        compiler_params=_params(("arbitrary",), VMEM_LIMIT_TABLE),
        name="peer_u",
    )(idx.reshape(n // tb, tb, PEER_PICKS), xw, tab)


def _peer_v(idx, wexp, tab):
    n = idx.shape[0]
    tb = min(PEER_TOKENS, n)
    out = pl.pallas_call(
        _peer_v_kernel,
        grid=(n // tb,),
        in_specs=[_idx_spec(tb),
                  pl.BlockSpec((ROW_CHUNKS, tb, LANES), lambda i: (0, i, 0)),
                  _table_spec(tab)],
        out_specs=pl.BlockSpec((tb, ROW_CHUNKS, LANES), lambda i: (i, 0, 0)),
        out_shape=jax.ShapeDtypeStruct((n, ROW_CHUNKS, LANES), F32),
        compiler_params=_params(("arbitrary",), VMEM_LIMIT_TABLE),
        name="peer_v",
    )(idx.reshape(n // tb, tb, PEER_PICKS), wexp, tab)
    return out.reshape(n * ROW_CHUNKS, LANES)


def _gate_act_kernel(a_ref, g_ref, e_ref, o_ref):
    a = a_ref[...]
    w = g_ref[...] * (0.5 * a * (1.0 + lax.erf(a * (2.0 ** -0.5))))
    wx = _dot(w.astype(BF16), e_ref[...])
    for j in range(ROW_CHUNKS):
        o_ref[j] = wx[:, j * LANES:(j + 1) * LANES]


def _gate_act(act, g):
    n = act.shape[0]
    tm = min(1024, n)
    spec = pl.BlockSpec((tm, LANES), lambda i: (i, 0))
    expand = (jnp.arange(D_MODEL)[None, :] // ROW_CHUNKS == jnp.arange(LANES)[:, None]).astype(BF16)
    return pl.pallas_call(
        _gate_act_kernel,
        grid=(n // tm,),
        in_specs=[spec, spec, pl.BlockSpec((LANES, D_MODEL), lambda i: (0, 0))],
        out_specs=pl.BlockSpec((ROW_CHUNKS, tm, LANES), lambda i: (0, i, 0)),
        out_shape=jax.ShapeDtypeStruct((ROW_CHUNKS, n, LANES), F32),
        compiler_params=_params(("parallel",)),
        name="peer_gate_act",
    )(act, g, expand)


def _peer(hb, xw, w_q, subkeys, u_tabs, v_tabs, layer):
    qp = _matmul(hb, w_q.astype(BF16))
    idx_t, g_t = _peer_topk(qp, subkeys.astype(BF16))
    idx = idx_t.T
    act = _peer_u(idx, xw, _pack_table(u_tabs, layer))
    return _peer_v(idx, _gate_act(act, g_t.T), _pack_table(v_tabs, layer))


def kernel(x, mem, a_w_in, a_w_gate2, a_b_gate, a_head_g, a_w_mem_kv, a_w_out, b_w_in, b_w_mem_kv,
           b_w_out, sb_w_kv, peer_w_q, peer_subkeys, peer_u, peer_v, ln_g, ln_b):
    batch, seq, d = x.shape
    n = batch * seq
    h = x.reshape(n, d)
    memf = mem.reshape(batch * N_MEM, d)
    pad_w = 2 * GLA_DV
    s = [GLA_K, 2 * GLA_K, 2 * GLA_K + GLA_V, 2 * GLA_K + 2 * GLA_V,
         2 * GLA_K + 2 * GLA_V + GLA_GATE_RANK]

    w_in = a_w_in[0]
    w_in_r = jnp.concatenate(
        [w_in[:, :s[3]], w_in[:, s[4]:], w_in[:, s[3]:s[4]],
         jnp.zeros((d, pad_w - GLA_GATE_RANK), F32)], axis=1).astype(BF16)
    proj = _matmul(h, w_in_r)
    wg = jnp.concatenate([a_w_gate2[0], jnp.zeros((pad_w - GLA_GATE_RANK, GLA_K), F32)],
                         axis=0).astype(BF16)
    o = _gla(proj, wg, a_b_gate[0].reshape(1, GLA_K), a_head_g[0].reshape(1, GLA_V), batch, seq)
    kv_mem = _matmul(memf, a_w_mem_kv[0].astype(BF16))
    m = _mem_attn(proj, (2 * GLA_K + 2 * GLA_V) // MEM_W, kv_mem, batch, seq)
    w_out = a_w_out[0].astype(BF16)
    h, hb, xw = _proj_ln(o, m, h, w_out[:GLA_V], w_out[GLA_V:], ln_g[0, 0].reshape(1, d),
                         ln_b[0, 0].reshape(1, d))
    ffn = _peer(hb, xw, peer_w_q[0], peer_subkeys[0], peer_u, peer_v, 0)
    h, hb = _res_ln(ffn, h, ln_g[0, 1].reshape(1, d), ln_b[0, 1].reshape(1, d))

    kv_sb = _matmul(hb, sb_w_kv.astype(BF16), out_dtype=BF16)
    proj = _matmul(hb, b_w_in[0].astype(BF16))
    o = _stick_breaking(proj, kv_sb, batch, seq)
    kv_mem = _matmul(memf, b_w_mem_kv[0].astype(BF16))
    m = _mem_attn(proj, SB_W // MEM_W, kv_mem, batch, seq)
    w_out = b_w_out[0].astype(BF16)
    h, hb, xw = _proj_ln(o, m, h, w_out[:SB_W], w_out[SB_W:], ln_g[1, 0].reshape(1, d),
                         ln_b[1, 0].reshape(1, d))
    ffn = _peer(hb, xw, peer_w_q[1], peer_subkeys[1], peer_u, peer_v, 1)
    h, _ = _res_ln(ffn, h, ln_g[1, 1].reshape(1, d), ln_b[1, 1].reshape(1, d))
    return h.reshape(batch, seq, d)
```

```python
import functools

import jax
import jax.numpy as jnp
from jax import lax
from jax.experimental import pallas as pl
from jax.experimental.pallas import tpu as pltpu

F32 = jnp.float32
BF16 = jnp.bfloat16
I32 = jnp.int32

LANES = 128
SUBLANES = 8
VMEM_LIMIT_DEFAULT = 48 * 1024 * 1024
VMEM_LIMIT_TABLE = 56 * 1024 * 1024

D_MODEL = 1024
N_MEM = 256
GLA_HEADS = 6
GLA_DK = 64
GLA_DV = 128
GLA_K = GLA_HEADS * GLA_DK
GLA_V = GLA_HEADS * GLA_DV
GLA_GATE_RANK = 16
GLA_TAU = 16.0
GLA_CHUNK = 64
SB_HEADS = 12
SB_DIM = 64
SB_W = SB_HEADS * SB_DIM
SB_BLOCK = 128
MEM_HEADS = 4
MEM_DIM = 64
MEM_W = MEM_HEADS * MEM_DIM
PEER_HEADS = 8
PEER_KEYS = 128
PEER_EXPERTS = PEER_KEYS * PEER_KEYS
PEER_TOPK = 16
PEER_QHALF = 128
PEER_PICKS = PEER_HEADS * PEER_TOPK
DEPTH = 2
DEEPNORM_ALPHA = (2.0 * DEPTH) ** 0.25
EPS = 1e-5

ROW_CHUNKS = D_MODEL // LANES
ROW_WORDS = ROW_CHUNKS // 2

NT_DIMS = (((1,), (1,)), ((), ()))
TN_DIMS = (((0,), (0,)), ((), ()))


def _params(semantics, vmem=VMEM_LIMIT_DEFAULT):
    return pltpu.CompilerParams(dimension_semantics=semantics, vmem_limit_bytes=vmem)


def _log_sigmoid(z):
    return jnp.minimum(z, 0.0) - jnp.log1p(jnp.exp(-jnp.abs(z)))


def _dot(a, b):
    return jnp.dot(a, b, preferred_element_type=F32)


def _split_bf16(x, parts):
    out = []
    r = x
    for _ in range(parts):
        p = r.astype(BF16)
        out.append(p)
        r = r - p.astype(F32)
    return out


def _layer_norm(y, g, b):
    mu = jnp.mean(y, axis=-1, keepdims=True)
    yc = y - mu
    var = jnp.mean(yc * yc, axis=-1, keepdims=True)
    return yc * lax.rsqrt(var + EPS) * g + b


def _matmul_kernel(x_ref, w_ref, o_ref):
    o_ref[...] = _dot(x_ref[...].astype(BF16), w_ref[...]).astype(o_ref.dtype)


MATMUL_ROWS = 512


def _matmul(x, w, out_dtype=F32):
    m, k = x.shape
    n = w.shape[1]
    tm = min(MATMUL_ROWS, m)
    return pl.pallas_call(
        _matmul_kernel,
        grid=(m // tm,),
        in_specs=[pl.BlockSpec((tm, k), lambda i: (i, 0)),
                  pl.BlockSpec((k, n), lambda i: (0, 0))],
        out_specs=pl.BlockSpec((tm, n), lambda i: (i, 0)),
        out_shape=jax.ShapeDtypeStruct((m, n), out_dtype),
        compiler_params=_params(("parallel",)),
        name="matmul",
    )(x, w)


GLA_ROWS = 256


def _gla_kernel(q_ref, k_ref, v_ref, r_ref, glr_ref, wg_ref, bg_ref, hg_ref, o_ref, st_ref):
    @pl.when(pl.program_id(2) == 0)
    def _():
        st_ref[...] = jnp.zeros_like(st_ref)

    c_rows = GLA_CHUNK
    lane = lax.broadcasted_iota(I32, (c_rows, LANES), 1)
    head_masks = (lane < GLA_DK, lane >= GLA_DK)
    row = lax.broadcasted_iota(I32, (c_rows, c_rows), 0)
    col = lax.broadcasted_iota(I32, (c_rows, c_rows), 1)
    causal = row >= col
    tril = causal.astype(BF16)
    wg = wg_ref[...]
    bg = bg_ref[...]
    hg = hg_ref[...]
    chunks = range(q_ref.shape[0] // c_rows)
    heads = range(2)
    rows = [slice(c * c_rows, (c + 1) * c_rows) for c in chunks]
    cols = [slice(a * GLA_DV, (a + 1) * GLA_DV) for a in heads]
    z = [_dot(glr_ref[rows[c], :].astype(BF16), wg) + bg for c in chunks]
    lg_parts = [_split_bf16(_log_sigmoid(z[c]) * (1.0 / GLA_TAU), 3) for c in chunks]
    bc = [sum(_dot(tril, part) for part in lg_parts[c]) for c in chunks]
    bl = [bc[c][c_rows - 1:c_rows, :] for c in chunks]
    qd = [q_ref[rows[c], :] * (GLA_DK ** -0.5) * jnp.exp(bc[c]) for c in chunks]
    ki = [(k_ref[rows[c], :] * jnp.exp(-bc[c])).astype(BF16) for c in chunks]
    ke = [k_ref[rows[c], :] * jnp.exp(bl[c] - bc[c]) for c in chunks]
    dec = [jnp.exp(bl[c]) for c in chunks]
    qa = [[jnp.where(head_masks[a], qd[c], 0.0).astype(BF16) for a in heads] for c in chunks]
    kea = [[jnp.where(head_masks[a], ke[c], 0.0).astype(BF16) for a in heads] for c in chunks]
    va = [[v_ref[rows[c], cols[a]].astype(BF16) for a in heads] for c in chunks]
    attn = [[jnp.where(causal, lax.dot_general(qa[c][a], ki[c], NT_DIMS, preferred_element_type=F32),
                       0.0).astype(BF16) for a in heads] for c in chunks]
    inc = [[lax.dot_general(va[c][a], kea[c][a], TN_DIMS, preferred_element_type=F32)
            for a in heads] for c in chunks]
    o_intra = [[_dot(attn[c][a], va[c][a]) for a in heads] for c in chunks]
    states = [st_ref[0], st_ref[1]]
    for c in chunks:
        for a in heads:
            o = o_intra[c][a] + lax.dot_general(qa[c][a], states[a].astype(BF16), NT_DIMS,
                                                preferred_element_type=F32)
            states[a] = states[a] * dec[c] + inc[c][a]
            ms = jnp.mean(o * o, axis=-1, keepdims=True)
            r = r_ref[rows[c], cols[a]]
            silu = r / (1.0 + jnp.exp(-r))
            o_ref[rows[c], cols[a]] = o * lax.rsqrt(ms + EPS) * hg[:, cols[a]] * silu
    st_ref[0] = states[0]
    st_ref[1] = states[1]


def _gla(proj, wg, bg, hg, batch, seq):
    n = proj.shape[0]
    tr = min(GLA_ROWS, seq)
    nt = seq // tr
    rowmap = lambda b, j, t: b * nt + t
    pair_w = 2 * GLA_DV
    return pl.pallas_call(
        _gla_kernel,
        grid=(batch, GLA_HEADS // 2, nt),
        in_specs=[
            pl.BlockSpec((tr, LANES), lambda b, j, t: (rowmap(b, j, t), j)),
            pl.BlockSpec((tr, LANES), lambda b, j, t: (rowmap(b, j, t), GLA_K // LANES + j)),
            pl.BlockSpec((tr, pair_w), lambda b, j, t: (rowmap(b, j, t), 2 * GLA_K // pair_w + j)),
            pl.BlockSpec((tr, pair_w),
                         lambda b, j, t: (rowmap(b, j, t), (2 * GLA_K + GLA_V) // pair_w + j)),
            pl.BlockSpec((tr, pair_w),
                         lambda b, j, t: (rowmap(b, j, t), (2 * GLA_K + 2 * GLA_V + MEM_W) // pair_w)),
            pl.BlockSpec((pair_w, LANES), lambda b, j, t: (0, j)),
            pl.BlockSpec((1, LANES), lambda b, j, t: (0, j)),
            pl.BlockSpec((1, pair_w), lambda b, j, t: (0, j)),
        ],
        out_specs=pl.BlockSpec((tr, pair_w), lambda b, j, t: (rowmap(b, j, t), j)),
        out_shape=jax.ShapeDtypeStruct((n, GLA_V), F32),
        scratch_shapes=[pltpu.VMEM((2, GLA_DV, LANES), F32)],
        compiler_params=_params(("parallel", "parallel", "arbitrary")),
        name="gla",
    )(proj, proj, proj, proj, proj, wg, bg, hg)


MEM_ROWS = 512


def _mem_attn_kernel(q_ref, kv_ref, o_ref):
    tq = q_ref.shape[0]
    lane = lax.broadcasted_iota(I32, (tq, LANES), 1)
    lo_head = lane < MEM_DIM
    for j in range(MEM_W // LANES):
        cols = slice(j * LANES, (j + 1) * LANES)
        q2 = q_ref[:, cols]
        k2 = kv_ref[:, cols].astype(BF16)
        v2 = kv_ref[:, MEM_W + j * LANES:MEM_W + (j + 1) * LANES].astype(BF16)
        outs = []
        for a in range(2):
            qa = jnp.where(lo_head if a == 0 else ~lo_head, q2, 0.0).astype(BF16)
            s = lax.dot_general(qa, k2, NT_DIMS, preferred_element_type=F32) * (MEM_DIM ** -0.5)
            e = jnp.exp(s - jnp.max(s, axis=-1, keepdims=True))
            p = e / jnp.sum(e, axis=-1, keepdims=True)
            outs.append(_dot(p.astype(BF16), v2))
        o_ref[:, cols] = jnp.where(lo_head, outs[0], outs[1])


def _mem_attn(proj, q_col_block, kv, batch, seq):
    n = proj.shape[0]
    tq = min(MEM_ROWS, seq)
    nt = seq // tq
    return pl.pallas_call(
        _mem_attn_kernel,
        grid=(batch, nt),
        in_specs=[pl.BlockSpec((tq, MEM_W), lambda b, t: (b * nt + t, q_col_block)),
                  pl.BlockSpec((N_MEM, 2 * MEM_W), lambda b, t: (b, 0))],
        out_specs=pl.BlockSpec((tq, MEM_W), lambda b, t: (b * nt + t, 0)),
        out_shape=jax.ShapeDtypeStruct((n, MEM_W), F32),
        compiler_params=_params(("parallel", "parallel")),
        name="mem_attn",
    )(proj, kv)


SB_TILE = 256


SB_MASKED = -1e30


def _sb_kernel(q_ref, k_ref, v_ref, o_ref, *scratch):
    acc_refs, run_refs = scratch[0:2], scratch[2:4]
    stage = (scratch[4:10], scratch[10:16])
    i = pl.program_id(2)
    t = q_ref.shape[0]
    heads = range(2)
    row = lax.broadcasted_iota(I32, (t, t), 0)
    col = lax.broadcasted_iota(I32, (t, t), 1)
    later = (row > col).astype(BF16)
    lo_head = lax.broadcasted_iota(I32, (t, LANES), 1) < SB_DIM
    q2 = q_ref[...] * (SB_DIM ** -0.5)
    qa = (jnp.where(lo_head, q2, 0.0).astype(BF16), jnp.where(lo_head, 0.0, q2).astype(BF16))
    for ref in acc_refs + run_refs:
        ref[...] = jnp.zeros_like(ref)

    def scores(j, p, diag):
        kj = k_ref[pl.ds(pl.multiple_of(j * t, t), t), :]
        z = [lax.dot_general(qa[a], kj, NT_DIMS, preferred_element_type=F32) for a in heads]
        for a in heads:
            ls = jnp.minimum(z[a], 0.0) - jnp.log(1.0 + jnp.exp(-jnp.abs(z[a])))
            ln = ls - z[a]
            if diag:
                visible = col < row
                ln = jnp.where(visible, ln, 0.0)
                ls = jnp.where(visible, ls, SB_MASKED)
            hi, lo = _split_bf16(ln, 2)
            stage[p][a][...] = ls
            stage[p][2 + a][...] = hi
            stage[p][4 + a][...] = lo

    def apply(j, p):
        vj = v_ref[pl.ds(pl.multiple_of(j * t, t), t), :]
        hi = [stage[p][2 + a][...] for a in heads]
        lo = [stage[p][4 + a][...] for a in heads]
        exc = [_dot(hi[a], later) + _dot(lo[a], later) for a in heads]
        run = [run_refs[a][...] for a in heads]
        a_w = [jnp.exp(stage[p][a][...] + exc[a] + jnp.concatenate([run[a]] * (t // LANES), axis=1))
               for a in heads]
        out = [_dot(a_w[a].astype(BF16), vj) for a in heads]
        for a in heads:
            acc_refs[a][...] += out[a]
            total = exc[a][:, 0:1] + hi[a][:, 0:1].astype(F32) + lo[a][:, 0:1].astype(F32)
            run_refs[a][...] = run[a] + jnp.broadcast_to(total, (t, LANES))

    scores(i, 0, True)

    def two_steps(m, carry):
        j = i - 2 * m
        scores(j - 1, 1, False)
        apply(j, 0)
        scores(j - 2, 0, False)
        apply(j - 1, 1)
        return carry

    lax.fori_loop(0, i // 2, two_steps, 0)

    @pl.when(i % 2 == 1)
    def _():
        scores(0, 1, False)
        apply(1, 0)
        apply(0, 1)

    @pl.when(i % 2 == 0)
    def _():
        apply(0, 0)

    o_ref[...] = jnp.where(lo_head, acc_refs[0][...], acc_refs[1][...])


def _stick_breaking(qproj, kv, batch, seq):
    n = qproj.shape[0]
    t = min(SB_TILE, seq)
    nq = seq // t
    pairs = SB_W // LANES
    return pl.pallas_call(
        _sb_kernel,
        grid=(batch, pairs, nq),
        in_specs=[pl.BlockSpec((t, LANES), lambda b, j, i: (b * nq + i, j)),
                  pl.BlockSpec((seq, LANES), lambda b, j, i: (b, j)),
                  pl.BlockSpec((seq, LANES), lambda b, j, i: (b, pairs + j))],
        out_specs=pl.BlockSpec((t, LANES), lambda b, j, i: (b * nq + i, j)),
        out_shape=jax.ShapeDtypeStruct((n, SB_W), F32),
        scratch_shapes=([pltpu.VMEM((t, LANES), F32)] * 4
                        + ([pltpu.VMEM((t, t), F32)] * 2 + [pltpu.VMEM((t, t), BF16)] * 4) * 2),
        compiler_params=_params(("parallel", "parallel", "arbitrary")),
        name="stick_breaking",
    )(qproj, kv, kv)


LN_ROWS = 256


def _proj_ln_kernel(a1_ref, a2_ref, h_ref, w1_ref, w2_ref, g_ref, b_ref, o_ref, ob_ref, xw_ref):
    mix = _dot(a1_ref[...].astype(BF16), w1_ref[...]) + _dot(a2_ref[...].astype(BF16), w2_ref[...])
    y = _layer_norm(DEEPNORM_ALPHA * h_ref[...] + mix, g_ref[...], b_ref[...])
    o_ref[...] = y
    ob_ref[...] = y.astype(BF16)
    _store_packed_rows(y, xw_ref)


def _ln_out(h, tm):
    spec = pl.BlockSpec((tm, h.shape[1]), lambda i: (i, 0))
    return [spec, spec], [jax.ShapeDtypeStruct(h.shape, F32), jax.ShapeDtypeStruct(h.shape, BF16)]


def _proj_ln(a1, a2, h, w1, w2, g, b):
    n = h.shape[0]
    tm = min(LN_ROWS, n)
    full = lambda arr: pl.BlockSpec(arr.shape, lambda i: (0, 0))
    rows = lambda arr: pl.BlockSpec((tm, arr.shape[1]), lambda i: (i, 0))
    out_specs, out_shape = _ln_out(h, tm)
    out_specs = out_specs + [pl.BlockSpec((tm * ROW_WORDS, LANES), lambda i: (i, 0))]
    out_shape = out_shape + [jax.ShapeDtypeStruct((n * ROW_WORDS, LANES), I32)]
    return pl.pallas_call(
        _proj_ln_kernel,
        grid=(n // tm,),
        in_specs=[rows(a1), rows(a2), rows(h), full(w1), full(w2), full(g), full(b)],
        out_specs=out_specs,
        out_shape=out_shape,
        compiler_params=_params(("parallel",)),
        name="proj_ln",
    )(a1, a2, h, w1, w2, g, b)


def _res_ln_kernel(f_ref, h_ref, g_ref, b_ref, o_ref, ob_ref):
    tm = h_ref.shape[0]
    f = jnp.concatenate([f_ref[pl.ds(c, tm, stride=ROW_CHUNKS), :] for c in range(ROW_CHUNKS)], axis=1)
    y = _layer_norm(DEEPNORM_ALPHA * h_ref[...] + f, g_ref[...], b_ref[...])
    o_ref[...] = y
    ob_ref[...] = y.astype(BF16)


def _res_ln(f, h, g, b):
    n = h.shape[0]
    tm = min(512, n)
    full = lambda arr: pl.BlockSpec(arr.shape, lambda i: (0, 0))
    rows = lambda arr: pl.BlockSpec((tm, arr.shape[1]), lambda i: (i, 0))
    out_specs, out_shape = _ln_out(h, tm)
    return pl.pallas_call(
        _res_ln_kernel,
        grid=(n // tm,),
        in_specs=[pl.BlockSpec((tm * ROW_CHUNKS, LANES), lambda i: (i, 0)), rows(h), full(g), full(b)],
        out_specs=out_specs,
        out_shape=out_shape,
        compiler_params=_params(("parallel",)),
        name="res_ln",
    )(f, h, g, b)


TOPK_TOKENS = 1024


def _top16_rows(problems):
    state = []
    for vals, payload in problems:
        k = vals.shape[0]
        rows = lax.broadcasted_iota(I32, (k // 2, vals.shape[1]), 0).astype(F32)
        first, second = vals[:k // 2], vals[k // 2:]
        keep = first >= second
        slot = {"hi": jnp.where(keep, first, second), "lo": jnp.where(keep, second, first),
                "row_hi": jnp.where(keep, rows, rows + float(k // 2)),
                "row_lo": jnp.where(keep, rows + float(k // 2), rows), "k": float(k)}
        if payload is not None:
            slot["pay_hi"] = jnp.where(keep, payload[:k // 2], payload[k // 2:])
            slot["pay_lo"] = jnp.where(keep, payload[k // 2:], payload[:k // 2])
        state.append(slot)
    outs = [([], []) for _ in problems]
    for _ in range(PEER_TOPK):
        for p, slot in enumerate(state):
            m = jnp.max(slot["hi"], axis=0, keepdims=True)
            tied_rows = jnp.where(slot["hi"] == m, slot["row_hi"], slot["k"])
            am = jnp.min(tied_rows, axis=0, keepdims=True)
            hit = tied_rows == am
            outs[p][0].append(m)
            if "pay_hi" in slot:
                outs[p][1].append(jnp.max(jnp.where(hit, slot["pay_hi"], -1.0), axis=0, keepdims=True))
                slot["pay_hi"] = jnp.where(hit, slot["pay_lo"], slot["pay_hi"])
            else:
                outs[p][1].append(am)
            slot["hi"] = jnp.where(hit, slot["lo"], slot["hi"])
            slot["row_hi"] = jnp.where(hit, slot["row_lo"], slot["row_hi"])
            slot["lo"] = jnp.where(hit, -jnp.inf, slot["lo"])
    return outs


def _pair_candidates(top0, top1):
    (s0, i0), (s1, i1) = top0, top1
    s0_all = jnp.concatenate(s0, axis=0)
    i0_all = jnp.concatenate(i0, axis=0)
    s1_all = jnp.concatenate(s1, axis=0)
    i1_all = jnp.concatenate(i1, axis=0)
    half = PEER_TOPK // 2
    sub = lax.broadcasted_iota(I32, (half, s0_all.shape[1]), 0)
    cand_s = [s0[0] + s1_all]
    cand_i = [i0[0] * float(PEER_KEYS) + i1_all]
    for i in range(1, half):
        cand_s.append(jnp.where(sub < PEER_TOPK // (i + 1), s0[i] + s1_all[:half], -jnp.inf))
        cand_i.append(i0[i] * float(PEER_KEYS) + i1_all[:half])
    cand_s.append(s0_all[half:] + s1[0])
    cand_i.append(i0_all[half:] * float(PEER_KEYS) + i1[0])
    return jnp.concatenate(cand_s, axis=0), jnp.concatenate(cand_i, axis=0)


def _peer_topk_kernel(q_ref, sk_ref, idx_ref, g_ref):
    groups = q_ref.shape[0] // LANES
    scores = []
    for grp in range(groups):
        for p in range(2):
            qp = q_ref[grp * LANES:(grp + 1) * LANES, p * PEER_QHALF:(p + 1) * PEER_QHALF]
            s_t = lax.dot_general(sk_ref[0, p], qp.astype(BF16), NT_DIMS, preferred_element_type=F32)
            scores.append((s_t, None))
    tops = _top16_rows(scores)
    best = _top16_rows([_pair_candidates(tops[2 * grp], tops[2 * grp + 1]) for grp in range(groups)])
    for grp, (best_s, best_i) in enumerate(best):
        cols = slice(grp * LANES, (grp + 1) * LANES)
        e = [jnp.exp(s - best_s[0]) for s in best_s]
        g_ref[:, cols] = jnp.concatenate(e, axis=0) / sum(e)
        idx_ref[:, cols] = (jnp.concatenate(best_i, axis=0) * float(ROW_WORDS)).astype(I32)


def _peer_topk(qp, subkeys):
    n = qp.shape[0]
    tt = min(TOPK_TOKENS, n)
    spec_out = pl.BlockSpec((PEER_TOPK, tt), lambda i, h: (h, i))
    return pl.pallas_call(
        _peer_topk_kernel,
        grid=(n // tt, PEER_HEADS),
        in_specs=[pl.BlockSpec((tt, 2 * PEER_QHALF), lambda i, h: (i, h)),
                  pl.BlockSpec((1, 2, PEER_KEYS, PEER_QHALF), lambda i, h: (h, 0, 0, 0))],
        out_specs=[spec_out, spec_out],
        out_shape=[jax.ShapeDtypeStruct((PEER_PICKS, n), I32),
                   jax.ShapeDtypeStruct((PEER_PICKS, n), F32)],
        compiler_params=_params(("parallel", "parallel")),
        name="peer_topk",
    )(qp, subkeys)


PACK_ROWS = 512


def _store_packed_rows(x, o_ref):
    r = x.shape[0]
    bits = lax.bitcast_convert_type(x.astype(BF16).astype(F32), I32)
    for s in range(ROW_WORDS):
        lo = lax.shift_right_logical(bits[:, (2 * s) * LANES:(2 * s + 1) * LANES], 16)
        hi = bits[:, (2 * s + 1) * LANES:(2 * s + 2) * LANES]
        o_ref[pl.ds(s, r, stride=ROW_WORDS), :] = hi | lo


def _pack_table_kernel(x_ref, o_ref):
    _store_packed_rows(x_ref[0], o_ref)


def _pack_table(tabs, layer):
    e = tabs.shape[1]
    tr = min(PACK_ROWS, e)
    return pl.pallas_call(
        _pack_table_kernel,
        grid=(e // tr,),
        in_specs=[pl.BlockSpec((1, tr, D_MODEL), lambda i: (layer, i, 0))],
        out_specs=pl.BlockSpec((tr * ROW_WORDS, LANES), lambda i: (i, 0)),
        out_shape=jax.ShapeDtypeStruct((e * ROW_WORDS, LANES), I32),
        compiler_params=_params(("parallel",)),
        name="pack_table",
    )(tabs)


PEER_TOKENS = 256
PEER_UNROLL = 64
PAIRS = PEER_PICKS // 2


def _gather_rows(idx_ref, tab_ref, n):
    pieces = []
    for k in range(PAIRS):
        e0 = idx_ref[0, n, 2 * k]
        e1 = idx_ref[0, n, 2 * k + 1]
        words = jnp.concatenate(
            [tab_ref[pl.ds(pl.multiple_of(e, ROW_WORDS), ROW_WORDS), :] for e in (e0, e1)],
            axis=0)
        pieces.append(pltpu.bitcast(words, BF16))
    return pieces


def _token_loop(n_tokens, group):
    def body(m, carry):
        for q in range(PEER_UNROLL // SUBLANES):
            group(pl.multiple_of(m * PEER_UNROLL + q * SUBLANES, SUBLANES))
        return carry

    lax.fori_loop(0, n_tokens // PEER_UNROLL, body, 0)


def _peer_u_kernel(idx_ref, x_ref, tab_ref, o_ref):
    ones = jnp.ones((LANES, LANES), BF16)
    lane = lax.broadcasted_iota(I32, (ROW_CHUNKS, LANES), 1)

    def token(n):
        x4 = x_ref[pl.ds(pl.multiple_of(n * ROW_WORDS, ROW_WORDS), ROW_WORDS), :]
        xw = pltpu.bitcast(jnp.concatenate([x4, x4], axis=0), BF16)
        prods = [piece * xw for piece in _gather_rows(idx_ref, tab_ref, n)]
        z = _dot(jnp.concatenate(prods, axis=0), ones)
        n_acc = 4
        acc = [jnp.zeros((ROW_CHUNKS, LANES), F32) for _ in range(n_acc)]
        for e in range(PEER_PICKS):
            acc[e % n_acc] = acc[e % n_acc] + jnp.where(
                lane == e, z[e * ROW_CHUNKS:(e + 1) * ROW_CHUNKS, :], 0.0)
        return jnp.sum(sum(acc), axis=0, keepdims=True)

    def group(base):
        o_ref[pl.ds(base, SUBLANES), :] = jnp.concatenate(
            [token(base + r) for r in range(SUBLANES)], axis=0)

    _token_loop(o_ref.shape[0], group)


def _peer_v_kernel(idx_ref, wexp_ref, tab_ref, o_ref):
    lane = lax.broadcasted_iota(I32, (ROW_CHUNKS, LANES), 1)
    sub = lax.broadcasted_iota(I32, (ROW_CHUNKS, LANES), 0)
    own_chunk = (lane & (ROW_CHUNKS - 1)) == sub

    def group(base):
        weights = [wexp_ref[j, pl.ds(base, SUBLANES), :] for j in range(ROW_CHUNKS)]
        for r in range(SUBLANES):
            n = base + r
            rows = jnp.concatenate(_gather_rows(idx_ref, tab_ref, n), axis=0)
            lhs = jnp.concatenate(
                [jnp.where(own_chunk, jnp.broadcast_to(weights[j][r:r + 1, :], (ROW_CHUNKS, LANES)), 0.0)
                 for j in range(ROW_CHUNKS)], axis=1).astype(BF16)
            o_ref[n] = _dot(lhs, rows)

    _token_loop(o_ref.shape[0], group)


def _idx_spec(tb):
    return pl.BlockSpec((1, tb, PEER_PICKS), lambda i: (i, 0, 0), memory_space=pltpu.SMEM)


def _table_spec(tab):
    return pl.BlockSpec(tab.shape, lambda i: (0, 0), pipeline_mode=pl.Buffered(1))


def _peer_u(idx, xw, tab):
    n = idx.shape[0]
    tb = min(PEER_TOKENS, n)
    return pl.pallas_call(
        _peer_u_kernel,
        grid=(n // tb,),
        in_specs=[_idx_spec(tb), pl.BlockSpec((tb * ROW_WORDS, LANES), lambda i: (i, 0)),
                  _table_spec(tab)],
        out_specs=pl.BlockSpec((tb, LANES), lambda i: (i, 0)),
        out_shape=jax.ShapeDtypeStruct((n, LANES), F32),
        compiler_params=_params(("arbitrary",), VMEM_LIMIT_TABLE),
        name="peer_u",
    )(idx.reshape(n // tb, tb, PEER_PICKS), xw, tab)


def _peer_v(idx, wexp, tab):
    n = idx.shape[0]
    tb = min(PEER_TOKENS, n)
    out = pl.pallas_call(
        _peer_v_kernel,
        grid=(n // tb,),
        in_specs=[_idx_spec(tb),
                  pl.BlockSpec((ROW_CHUNKS, tb, LANES), lambda i: (0, i, 0)),
                  _table_spec(tab)],
        out_specs=pl.BlockSpec((tb, ROW_CHUNKS, LANES), lambda i: (i, 0, 0)),
        out_shape=jax.ShapeDtypeStruct((n, ROW_CHUNKS, LANES), F32),
        compiler_params=_params(("arbitrary",), VMEM_LIMIT_TABLE),
        name="peer_v",
    )(idx.reshape(n // tb, tb, PEER_PICKS), wexp, tab)
    return out.reshape(n * ROW_CHUNKS, LANES)


def _gate_act_kernel(a_ref, g_ref, e_ref, o_ref):
    a = a_ref[...]
    w = g_ref[...] * (0.5 * a * (1.0 + lax.erf(a * (2.0 ** -0.5))))
    wx = _dot(w.astype(BF16), e_ref[...])
    for j in range(ROW_CHUNKS):
        o_ref[j] = wx[:, j * LANES:(j + 1) * LANES]


def _gate_act(act, g):
    n = act.shape[0]
    tm = min(1024, n)
    spec = pl.BlockSpec((tm, LANES), lambda i: (i, 0))
    expand = (jnp.arange(D_MODEL)[None, :] // ROW_CHUNKS == jnp.arange(LANES)[:, None]).astype(BF16)
    return pl.pallas_call(
        _gate_act_kernel,
        grid=(n // tm,),
        in_specs=[spec, spec, pl.BlockSpec((LANES, D_MODEL), lambda i: (0, 0))],
        out_specs=pl.BlockSpec((ROW_CHUNKS, tm, LANES), lambda i: (0, i, 0)),
        out_shape=jax.ShapeDtypeStruct((ROW_CHUNKS, n, LANES), F32),
        compiler_params=_params(("parallel",)),
        name="peer_gate_act",
    )(act, g, expand)


def _peer(hb, xw, w_q, subkeys, u_tabs, v_tabs, layer):
    qp = _matmul(hb, w_q.astype(BF16))
    idx_t, g_t = _peer_topk(qp, subkeys.astype(BF16))
    idx = idx_t.T
    act = _peer_u(idx, xw, _pack_table(u_tabs, layer))
    return _peer_v(idx, _gate_act(act, g_t.T), _pack_table(v_tabs, layer))


def kernel(x, mem, a_w_in, a_w_gate2, a_b_gate, a_head_g, a_w_mem_kv, a_w_out, b_w_in, b_w_mem_kv,
           b_w_out, sb_w_kv, peer_w_q, peer_subkeys, peer_u, peer_v, ln_g, ln_b):
    batch, seq, d = x.shape
    n = batch * seq
    h = x.reshape(n, d)
    memf = mem.reshape(batch * N_MEM, d)
    pad_w = 2 * GLA_DV
    s = [GLA_K, 2 * GLA_K, 2 * GLA_K + GLA_V, 2 * GLA_K + 2 * GLA_V,
         2 * GLA_K + 2 * GLA_V + GLA_GATE_RANK]

    w_in = a_w_in[0]
    w_in_r = jnp.concatenate(
        [w_in[:, :s[3]], w_in[:, s[4]:], w_in[:, s[3]:s[4]],
         jnp.zeros((d, pad_w - GLA_GATE_RANK), F32)], axis=1).astype(BF16)
    proj = _matmul(h, w_in_r)
    wg = jnp.concatenate([a_w_gate2[0], jnp.zeros((pad_w - GLA_GATE_RANK, GLA_K), F32)],
                         axis=0).astype(BF16)
    o = _gla(proj, wg, a_b_gate[0].reshape(1, GLA_K), a_head_g[0].reshape(1, GLA_V), batch, seq)
    kv_mem = _matmul(memf, a_w_mem_kv[0].astype(BF16))
    m = _mem_attn(proj, (2 * GLA_K + 2 * GLA_V) // MEM_W, kv_mem, batch, seq)
    w_out = a_w_out[0].astype(BF16)
    h, hb, xw = _proj_ln(o, m, h, w_out[:GLA_V], w_out[GLA_V:], ln_g[0, 0].reshape(1, d),
                         ln_b[0, 0].reshape(1, d))
    ffn = _peer(hb, xw, peer_w_q[0], peer_subkeys[0], peer_u, peer_v, 0)
    h, hb = _res_ln(ffn, h, ln_g[0, 1].reshape(1, d), ln_b[0, 1].reshape(1, d))

    kv_sb = _matmul(hb, sb_w_kv.astype(BF16), out_dtype=BF16)
    proj = _matmul(hb, b_w_in[0].astype(BF16))
    o = _stick_breaking(proj, kv_sb, batch, seq)
    kv_mem = _matmul(memf, b_w_mem_kv[0].astype(BF16))
    m = _mem_attn(proj, SB_W // MEM_W, kv_mem, batch, seq)
    w_out = b_w_out[0].astype(BF16)
    h, hb, xw = _proj_ln(o, m, h, w_out[:SB_W], w_out[SB_W:], ln_g[1, 0].reshape(1, d),
                         ln_b[1, 0].reshape(1, d))
    ffn = _peer(hb, xw, peer_w_q[1], peer_subkeys[1], peer_u, peer_v, 1)
    h, _ = _res_ln(ffn, h, ln_g[1, 1].reshape(1, d), ln_b[1, 1].reshape(1, d))
    return h.reshape(batch, seq, d)
```
